```python
import jax
import jax.numpy as jnp
from jax import lax
import numpy as np

D_MODEL = 1024
BATCH = 8
SEQ = 4096
DEPTH = 1

GRID_W = 64
CTX_LEN = 256
EPS = 1e-6
MLA_HEADS = 8
MLA_NOPE = 64
MLA_ROPE = 32
MLA_V_HEAD = 64
MLA_Q_LORA = 384
MLA_KV_LORA = 256
MLA_WIDTH = MLA_HEADS * MLA_V_HEAD
MLA_SCALE = (MLA_NOPE + MLA_ROPE) ** -0.5
ROPE_AXIS_DIM = MLA_ROPE // 2
ROPE_THETA = 10000.0
Q_BLOCK = 128
GLA_HEADS = 4
GLA_HEAD_K = 128
GLA_HEAD_V = 128
GLA_KEY = GLA_HEADS * GLA_HEAD_K
GLA_VALUE = GLA_HEADS * GLA_HEAD_V
GLA_QSCALE = GLA_HEAD_K ** -0.5
GATE_RANK = 16
GATE_NORM = 16.0
GLA_CHUNK = 64
D_FF = 2816
CONV_W = 3
IN_SPLITS = (MLA_Q_LORA, MLA_KV_LORA, MLA_ROPE, GLA_KEY, GLA_KEY, GLA_VALUE, GLA_VALUE, 2 * GATE_RANK, 2 * D_MODEL)
IN_COLS = sum(IN_SPLITS)

kernel_name = "hybrid_mla_gla_convglu_dit"


def rms_norm(x, g):
    xf = x.astype(jnp.float32)
    y = xf * lax.rsqrt(jnp.mean(xf * xf, axis=-1, keepdims=True) + EPS)
    return (y * g.astype(jnp.float32)).astype(x.dtype)


def modulate(h, shift, scale):
    return h * (1 + scale) + shift


def split_columns(z):
    offsets = [int(o) for o in np.cumsum(IN_SPLITS)[:-1]]
    return jnp.split(z, offsets, axis=-1)


def heads(z, n_heads):
    return z.reshape(z.shape[:-1] + (n_heads, z.shape[-1] // n_heads))


def axial_angles(length):
    t = jnp.arange(length, dtype=jnp.int32)
    row = (t // GRID_W).astype(jnp.float32)
    col = (t % GRID_W).astype(jnp.float32)
    inv_freq = ROPE_THETA ** (-jnp.arange(0, ROPE_AXIS_DIM, 2, dtype=jnp.float32) / ROPE_AXIS_DIM)
    return row[:, None] * inv_freq, col[:, None] * inv_freq


def rotate_axis(x, ang):
    x1, x2 = jnp.split(x, 2, axis=-1)
    cos, sin = jnp.cos(ang), jnp.sin(ang)
    return jnp.concatenate([x1 * cos - x2 * sin, x2 * cos + x1 * sin], axis=-1)


def rope_2d(x, ang_row, ang_col):
    xf = x.astype(jnp.float32)
    xr, xc = xf[..., :ROPE_AXIS_DIM], xf[..., ROPE_AXIS_DIM:]
    return jnp.concatenate([rotate_axis(xr, ang_row), rotate_axis(xc, ang_col)], axis=-1).astype(x.dtype)


def mla_queries(q_c, q_norm_g, w_uq):
    q = heads(rms_norm(q_c, q_norm_g) @ w_uq, MLA_HEADS)
    return q[..., :MLA_NOPE], q[..., MLA_NOPE:]


def mla_keys_values(kv_c, kv_norm_g, w_ukv):
    kv = heads(rms_norm(kv_c, kv_norm_g) @ w_ukv, MLA_HEADS)
    return kv[..., :MLA_NOPE], kv[..., MLA_NOPE:]


def attention(q_nope, q_rope, k_nope, k_rope, v):
    b, lq = q_nope.shape[:2]
    nb = lq // Q_BLOCK

    def blocks(z):
        return jnp.moveaxis(z.reshape((b, nb, Q_BLOCK) + z.shape[2:]), 1, 0)

    def one_block(qs):
        qn, qr = qs
        s = jnp.einsum('bqhd,bkhd->bhqk', qn, k_nope) + jnp.einsum('bqhr,bkr->bhqk', qr, k_rope)
        p = jax.nn.softmax(s.astype(jnp.float32) * MLA_SCALE, axis=-1).astype(v.dtype)
        return jnp.einsum('bhqk,bkhd->bqhd', p, v)

    o = lax.map(one_block, (blocks(q_nope), blocks(q_rope)))
    return jnp.moveaxis(o, 0, 1).reshape(b, lq, MLA_WIDTH)


def to_chunks(z):
    b, l, h, d = z.shape
    return z.reshape(b, l // GLA_CHUNK, GLA_CHUNK, h, d).transpose(0, 3, 1, 2, 4)


def from_chunks(z):
    b, h, n, c, d = z.shape
    return z.transpose(0, 2, 3, 1, 4).reshape(b, n * c, h, d)


def chunk_state_terms(kc, vc, b_cum):
    b_last = b_cum[..., -1:, :]
    u = jnp.einsum('bhncd,bhncv->bhndv', kc * jnp.exp(b_last - b_cum), vc)
    decay = jnp.exp(b_last[..., 0, :])
    return decay, u


def scan_states(decay, u, s0):
    def step(s, inp):
        d, du = inp
        return d[..., None] * s + du, s

    s_final, s_in = lax.scan(step, s0, (jnp.moveaxis(decay, 2, 0), jnp.moveaxis(u, 2, 0)))
    return s_final, jnp.moveaxis(s_in, 0, 2)


def gla_direction(q, k, v, g, s0):
    qc, kc, vc = (to_chunks(z.astype(jnp.float32)) for z in (q, k, v))
    b_cum = jnp.cumsum(to_chunks(g), axis=3)
    decay, u = chunk_state_terms(kc, vc, b_cum)
    s_final, s_in = scan_states(decay, u, s0)
    qe = qc * jnp.exp(b_cum)
    mask = jnp.tril(jnp.ones((GLA_CHUNK, GLA_CHUNK), jnp.float32))
    a = jnp.einsum('bhncd,bhnsd->bhncs', qe, kc * jnp.exp(-b_cum)) * mask
    o = jnp.einsum('bhncs,bhnsv->bhncv', a, vc) + jnp.einsum('bhncd,bhndv->bhncv', qe, s_in)
    return from_chunks(o), s_final


def gla_final_state(k, v, g, s0):
    kc, vc = to_chunks(k.astype(jnp.float32)), to_chunks(v.astype(jnp.float32))
    decay, u = chunk_state_terms(kc, vc, jnp.cumsum(to_chunks(g), axis=3))
    s_final, _ = scan_states(decay, u, s0)
    return s_final


def flip(z):
    return jnp.flip(z, axis=1)


def gla_bidirectional(q, k, v, g_f, g_b, s_f, s_b):
    o_f, s_f_out = gla_direction(q, k, v, g_f, s_f)
    o_b, s_b_out = gla_direction(flip(q), flip(k), flip(v), flip(g_b), s_b)
    return o_f + flip(o_b), s_f_out, s_b_out


def gla_inputs(gq, gk, gv, glow, w_decay, b_decay):
    lf, lb = glow[..., :GATE_RANK], glow[..., GATE_RANK:]
    g_f = jax.nn.log_sigmoid((lf @ w_decay[0] + b_decay[0]).astype(jnp.float32)) / GATE_NORM
    g_b = jax.nn.log_sigmoid((lb @ w_decay[1] + b_decay[1]).astype(jnp.float32)) / GATE_NORM
    return (heads(gq, GLA_HEADS) * GLA_QSCALE, heads(gk, GLA_HEADS), heads(gv, GLA_HEADS),
            heads(g_f, GLA_HEADS), heads(g_b, GLA_HEADS))


def gla_branch(o, r, norm_g, w_br):
    y = rms_norm(o, norm_g).reshape(o.shape[0], o.shape[1], GLA_VALUE).astype(r.dtype) * jax.nn.silu(r)
    return y @ w_br


def merge_branches(br_mla, br_gla, gate_logits, w_out):
    g = jax.nn.sigmoid(gate_logits.astype(jnp.float32)).astype(br_mla.dtype)
    g_mla, g_gla = jnp.split(g, 2, axis=-1)
    return (g_mla * br_mla + g_gla * br_gla) @ w_out


def token_mixer(h, hc, w_in, q_norm_g, w_uq, kv_norm_g, w_ukv, w_decay, b_decay, gla_norm_g,
                w_br_mla, w_br_gla, w_out, ang_row, ang_col, update_ctx):
    q_c, kv_c, k_rope, gq, gk, gv, gr, glow, gate_logits = split_columns(h @ w_in)
    q_cc, kv_cc, k_rope_c, gq_c, gk_c, gv_c, gr_c, glow_c, gate_logits_c = split_columns(hc @ w_in)

    k_nope_c, v_c = mla_keys_values(kv_cc, kv_norm_g, w_ukv)
    k_nope, v = mla_keys_values(kv_c, kv_norm_g, w_ukv)
    q_nope, q_rope = mla_queries(q_c, q_norm_g, w_uq)
    q_rope = rope_2d(q_rope, ang_row[:, None, :], ang_col[:, None, :])
    k_rope = rope_2d(k_rope, ang_row, ang_col)
    o_mla = attention(q_nope, q_rope,
                      jnp.concatenate([k_nope, k_nope_c], axis=1),
                      jnp.concatenate([k_rope, k_rope_c], axis=1),
                      jnp.concatenate([v, v_c], axis=1))

    b = h.shape[0]
    s0 = jnp.zeros((b, GLA_HEADS, GLA_HEAD_K, GLA_HEAD_V), jnp.float32)
    qg_c, kg_c, vg_c, gf_c, gb_c = gla_inputs(gq_c, gk_c, gv_c, glow_c, w_decay, b_decay)
    qg, kg, vg, gf, gb = gla_inputs(gq, gk, gv, glow, w_decay, b_decay)
    if update_ctx:
        o_gla_c, s_f, s_b = gla_bidirectional(qg_c, kg_c, vg_c, gf_c, gb_c, s0, s0)
    else:
        s_f = gla_final_state(kg_c, vg_c, gf_c, s0)
        s_b = gla_final_state(flip(kg_c), flip(vg_c), flip(gb_c), s0)
    o_gla, _, _ = gla_bidirectional(qg, kg, vg, gf, gb, s_f, s_b)

    out_lat = merge_branches(o_mla @ w_br_mla, gla_branch(o_gla, gr, gla_norm_g, w_br_gla), gate_logits, w_out)
    out_ctx = None
    if update_ctx:
        q_nope_c, q_rope_c = mla_queries(q_cc, q_norm_g, w_uq)
        o_mla_c = attention(q_nope_c, q_rope_c, k_nope_c, k_rope_c, v_c)
        out_ctx = merge_branches(o_mla_c @ w_br_mla, gla_branch(o_gla_c, gr_c, gla_norm_g, w_br_gla),
                                 gate_logits_c, w_out)
    return out_lat, out_ctx


def depthwise_conv_grid(u, w, bias, rows, cols):
    b, l, ch = u.shape
    grid = u.reshape(b, rows, cols, ch)
    y = lax.conv_general_dilated(grid, w[:, :, None, :].astype(u.dtype), (1, 1), 'SAME',
                                 dimension_numbers=('NHWC', 'HWIO', 'NHWC'), feature_group_count=ch)
    return y.reshape(b, l, ch) + bias


def conv_ffn(h, w_up, conv_w, conv_b, w_down, rows, cols):
    val, gate = jnp.split(h @ w_up, 2, axis=-1)
    gate = depthwise_conv_grid(gate, conv_w, conv_b, rows, cols)
    return (jax.nn.gelu(gate, approximate=False) * val) @ w_down


def setup_inputs(seed: int = 0) -> dict:
    key = jax.random.key(seed)
    ks = jax.random.split(key, 24)
    f32 = jnp.float32

    def normal(k, shape, s=1.0):
        return s * jax.random.normal(k, shape, f32)

    def dense(k, shape, fan_in, gain=1.0):
        return normal(k, shape, gain * fan_in ** -0.5)

    def norm_gain(k, shape):
        return 1.0 + normal(k, shape, 0.05)

    return {
        "x": normal(ks[0], (BATCH, SEQ, D_MODEL)),
        "c": normal(ks[1], (BATCH, D_MODEL)),
        "ctx": normal(ks[2], (BATCH, CTX_LEN, D_MODEL)),
        "c_ctx": normal(ks[3], (D_MODEL,)),
        "w_ada": dense(ks[4], (DEPTH, D_MODEL, 6 * D_MODEL), D_MODEL, 0.5),
        "b_ada": normal(ks[5], (DEPTH, 6 * D_MODEL), 0.01),
        "norm1_g": norm_gain(ks[6], (DEPTH, D_MODEL)),
        "w_in": dense(ks[7], (DEPTH, D_MODEL, IN_COLS), D_MODEL),
        "q_norm_g": norm_gain(ks[8], (DEPTH, MLA_Q_LORA)),
        "w_uq": dense(ks[9], (DEPTH, MLA_Q_LORA, MLA_HEADS * (MLA_NOPE + MLA_ROPE)), MLA_Q_LORA),
        "kv_norm_g": norm_gain(ks[10], (DEPTH, MLA_KV_LORA)),
        "w_ukv": dense(ks[11], (DEPTH, MLA_KV_LORA, MLA_HEADS * (MLA_NOPE + MLA_V_HEAD)), MLA_KV_LORA),
        "gla_w_decay": dense(ks[12], (DEPTH, 2, GATE_RANK, GLA_KEY), GATE_RANK),
        "gla_b_decay": normal(ks[13], (DEPTH, 2, GLA_KEY), 0.1),
        "gla_norm_g": norm_gain(ks[14], (DEPTH, GLA_HEAD_V)),
        "w_br_mla": dense(ks[15], (DEPTH, MLA_WIDTH, D_MODEL), MLA_WIDTH),
        "w_br_gla": dense(ks[16], (DEPTH, GLA_VALUE, D_MODEL), GLA_VALUE),
        "w_out": dense(ks[17], (DEPTH, D_MODEL, D_MODEL), D_MODEL),
        "norm2_g": norm_gain(ks[18], (DEPTH, D_MODEL)),
        "w_up": dense(ks[19], (DEPTH, D_MODEL, 2 * D_FF), D_MODEL),
        "conv_w": dense(ks[20], (DEPTH, CONV_W, CONV_W, D_FF), CONV_W * CONV_W),
        "conv_b": normal(ks[21], (DEPTH, D_FF), 0.01),
        "w_down": dense(ks[22], (DEPTH, D_FF, D_MODEL), D_FF),
        "final_g": norm_gain(ks[23], (D_MODEL,)),
    }


def reference(x, c, ctx, c_ctx, w_ada, b_ada, norm1_g, w_in, q_norm_g, w_uq, kv_norm_g, w_ukv,
              gla_w_decay, gla_b_decay, gla_norm_g, w_br_mla, w_br_gla, w_out, norm2_g, w_up,
              conv_w, conv_b, w_down, final_g):
    length = x.shape[1]
    rows = length // GRID_W
    ang_row, ang_col = axial_angles(length)
    x_lat, x_ctx = x, ctx
    for layer in range(DEPTH):
        update_ctx = layer + 1 < DEPTH
        mod = jax.nn.silu(c) @ w_ada[layer] + b_ada[layer]
        mod_c = jax.nn.silu(c_ctx) @ w_ada[layer] + b_ada[layer]
        sh1, sc1, gt1, sh2, sc2, gt2 = jnp.split(mod[:, None, :], 6, axis=-1)
        sh1c, sc1c, gt1c, sh2c, sc2c, gt2c = jnp.split(mod_c, 6, axis=-1)

        h = modulate(rms_norm(x_lat, norm1_g[layer]), sh1, sc1)
        hc = modulate(rms_norm(x_ctx, norm1_g[layer]), sh1c, sc1c)
        mix, mix_c = token_mixer(h, hc, w_in[layer], q_norm_g[layer], w_uq[layer], kv_norm_g[layer],
                                 w_ukv[layer], gla_w_decay[layer], gla_b_decay[layer], gla_norm_g[layer],
                                 w_br_mla[layer], w_br_gla[layer], w_out[layer], ang_row, ang_col, update_ctx)
        x_lat = x_lat + gt1 * mix
        h = modulate(rms_norm(x_lat, norm2_g[layer]), sh2, sc2)
        x_lat = x_lat + gt2 * conv_ffn(h, w_up[layer], conv_w[layer], conv_b[layer], w_down[layer], rows, GRID_W)
        if update_ctx:
            x_ctx = x_ctx + gt1c * mix_c
            hc = modulate(rms_norm(x_ctx, norm2_g[layer]), sh2c, sc2c)
            x_ctx = x_ctx + gt2c * conv_ffn(hc, w_up[layer], conv_w[layer], conv_b[layer], w_down[layer],
                                            1, x_ctx.shape[1])
    return rms_norm(x_lat, final_g)
```

```python
import functools

import numpy as np
import jax
import jax.numpy as jnp
from jax import lax
from jax.experimental import pallas as pl
from jax.experimental.pallas import tpu as pltpu

F32 = jnp.float32
BF16 = jnp.bfloat16

D_MODEL = 1024
GRID_W = 64
EPS = 1e-6
MLA_HEADS = 8
MLA_NOPE = 64
MLA_ROPE = 32
MLA_V_HEAD = 64
MLA_Q_LORA = 384
MLA_KV_LORA = 256
MLA_WIDTH = MLA_HEADS * MLA_V_HEAD
MLA_SCALE = (MLA_NOPE + MLA_ROPE) ** -0.5
ROPE_AXIS_DIM = MLA_ROPE // 2
ROPE_THETA = 10000.0
GLA_HEADS = 4
GLA_HEAD_K = 128
GLA_HEAD_V = 128
GLA_KEY = GLA_HEADS * GLA_HEAD_K
GLA_VALUE = GLA_HEADS * GLA_HEAD_V
GLA_QSCALE = GLA_HEAD_K ** -0.5
GATE_RANK = 16
GATE_NORM = 16.0
GLA_CHUNK = 64
D_FF = 2816
HEAD_PAD = 128

VMEM_LIMIT = 56 * 1024 * 1024

A_Q = (0, 384)
A_KV = (384, 640)
A_MISC = (640, 768)
A_GQ = (768, 1280)
A_GK = (1280, 1792)
A_GV = (1792, 2304)
A_COLS = 2304


def _dot(a, b):
    return jnp.dot(a, b, preferred_element_type=F32)


def _dot_nt(a, b):
    return lax.dot_general(a, b, (((1,), (1,)), ((), ())), preferred_element_type=F32)


def _dot_tn(a, b):
    return lax.dot_general(a, b, (((0,), (0,)), ((), ())), preferred_element_type=F32)


def _rms(x, g):
    return x * lax.rsqrt(jnp.mean(x * x, axis=-1, keepdims=True) + EPS) * g


def _const_spec(shape):
    nd = len(shape)
    return pl.BlockSpec(shape, lambda *_: (0,) * nd)


def _params(n_grid):
    return pltpu.CompilerParams(dimension_semantics=("arbitrary",) * n_grid, vmem_limit_bytes=VMEM_LIMIT)


def _ada_kernel(c_ref, w_ref, b_ref, o_ref):
    c = c_ref[...]
    s = (c * jax.nn.sigmoid(c)).astype(BF16)
    o_ref[...] = _dot(s, w_ref[...].astype(BF16)) + b_ref[...]


def _ada(c16, w_ada, b_ada):
    n = w_ada.shape[1]
    bn = 512
    return pl.pallas_call(
        _ada_kernel,
        grid=(n // bn,),
        in_specs=[_const_spec((16, D_MODEL)),
                  pl.BlockSpec((D_MODEL, bn), lambda j: (0, j)),
                  pl.BlockSpec((1, bn), lambda j: (0, j))],
        out_specs=pl.BlockSpec((16, bn), lambda j: (0, j)),
        out_shape=jax.ShapeDtypeStruct((16, n), F32),
        compiler_params=_params(1),
        name="ada",
    )(c16, w_ada, b_ada.reshape(1, n))


def _inproj_kernel(x_ref, sh_ref, sc_ref, g1_ref, tab_ref, w_ref, qg_ref, wuq_ref, kvg_ref, wkv_ref,
                   wd_ref, bd_ref, *out_refs, with_q):
    if with_q:
        q_ref, k_ref, v_ref, gq_ref, gk_ref, gv_ref, g_ref = out_refs
    else:
        k_ref, v_ref, gk_ref, gv_ref, g_ref = out_refs
    x = x_ref[0]
    h = (_rms(x, g1_ref[...]) * (1.0 + sc_ref[0]) + sh_ref[0]).astype(BF16)

    if with_q:
        zq = _dot(h, w_ref[:, A_Q[0]:A_Q[1]])
        qn = _rms(zq, qg_ref[...]).astype(BF16)
        qf = _dot(qn, wuq_ref[...])
        tq = tab_ref[:, 0:HEAD_PAD]
        for hh in range(MLA_HEADS):
            sl = slice(hh * HEAD_PAD, (hh + 1) * HEAD_PAD)
            q_ref[0, :, sl] = (qf[:, sl] * tq).astype(BF16)
        gq_ref[0] = (_dot(h, w_ref[:, A_GQ[0]:A_GQ[1]]) * GLA_QSCALE).astype(BF16)

    zkv = _dot(h, w_ref[:, A_KV[0]:A_KV[1]])
    kvn = _rms(zkv, kvg_ref[...]).astype(BF16)
    misc = _dot(h, w_ref[:, A_MISC[0]:A_MISC[1]])
    kr = (misc * tab_ref[:, 128:256]
          + pltpu.roll(misc, HEAD_PAD - 8, 1) * tab_ref[:, 256:384]
          + pltpu.roll(misc, 8, 1) * tab_ref[:, 384:512])
    kvf = _dot(jnp.concatenate([kvn, kr.astype(BF16)], axis=1), wkv_ref[...])
    k_ref[0] = kvf[:, :MLA_HEADS * HEAD_PAD].astype(BF16)
    v_ref[0] = kvf[:, MLA_HEADS * HEAD_PAD:].astype(BF16)

    gk_ref[0] = _dot(h, w_ref[:, A_GK[0]:A_GK[1]]).astype(BF16)
    gv_ref[0] = _dot(h, w_ref[:, A_GV[0]:A_GV[1]]).astype(BF16)

    gp = _dot(misc.astype(BF16), wd_ref[...]) + bd_ref[...]
    g_ref[0] = (jnp.minimum(gp, 0.0) - jnp.log1p(jnp.exp(-jnp.abs(gp)))) * (1.0 / GATE_NORM)


def _inproj(x, mod3, mod_row, g1, tab, w_a, qg, wuq, kvg, wkv, wd, bd, *, tm, with_q):
    b, l, _ = x.shape
    tm = min(tm, l)
    row = (lambda bi: bi) if mod_row is None else (lambda bi: mod_row)
    tok = lambda w: pl.BlockSpec((1, tm, w), lambda bi, i: (bi, i, 0))
    in_specs = [
        tok(D_MODEL),
        pl.BlockSpec((1, 1, D_MODEL), lambda bi, i: (row(bi), 0, 0)),
        pl.BlockSpec((1, 1, D_MODEL), lambda bi, i: (row(bi), 0, 1)),
        _const_spec((1, D_MODEL)),
        pl.BlockSpec((tm, 512), lambda bi, i: (i, 0)),
        _const_spec(w_a.shape), _const_spec(qg.shape), _const_spec(wuq.shape), _const_spec(kvg.shape),
        _const_spec(wkv.shape), _const_spec(wd.shape), _const_spec(bd.shape),
    ]
    sds = lambda w, dt: jax.ShapeDtypeStruct((b, l, w), dt)
    kw = MLA_HEADS * HEAD_PAD
    outs = [(kw, BF16), (MLA_WIDTH, BF16), (GLA_KEY, BF16), (GLA_VALUE, BF16), (2 * GLA_KEY, F32)]
    if with_q:
        outs = [(kw, BF16)] + outs[:2] + [(GLA_KEY, BF16)] + outs[2:]
    return pl.pallas_call(
        functools.partial(_inproj_kernel, with_q=with_q),
        grid=(b, l // tm),
        in_specs=in_specs,
        out_specs=[tok(w) for w, _ in outs],
        out_shape=[sds(w, dt) for w, dt in outs],
        compiler_params=_params(2),
        name="inproj_lat" if with_q else "inproj_ctx",
    )(x, mod3, mod3, g1, tab, w_a, qg, wuq, kvg, wkv, wd, bd)


def _attn_kernel(q_ref, kl_ref, kc_ref, vl_ref, vc_ref, o_ref, *, tk):
    tq = q_ref.shape[1]
    n_lat = kl_ref.shape[1] // tk
    q0 = q_ref[0, :, 0:HEAD_PAD]
    q1 = q_ref[0, :, HEAD_PAD:2 * HEAD_PAD]
    first = lax.broadcasted_iota(jnp.int32, (tq, 2 * MLA_V_HEAD), 1) < MLA_V_HEAD

    def head(q, k_t, v_t, m, l):
        s = _dot_nt(q, k_t)
        m_new = jnp.maximum(m, jnp.max(s, axis=-1, keepdims=True))
        a = jnp.exp(m - m_new)
        p = jnp.exp(s - m_new)
        l_new = a * l + jnp.sum(p, axis=-1, keepdims=True)
        return m_new, l_new, a, _dot(p.astype(BF16), v_t)

    def step(carry, k_t, v_t):
        m0, l0, m1, l1, acc = carry
        m0, l0, a0, pv0 = head(q0, k_t[:, 0:HEAD_PAD], v_t, m0, l0)
        m1, l1, a1, pv1 = head(q1, k_t[:, HEAD_PAD:], v_t, m1, l1)
        acc = acc * jnp.where(first, a0, a1) + jnp.where(first, pv0, pv1)
        return m0, l0, m1, l1, acc

    def lat_step(i, carry):
        r = pl.multiple_of(i * tk, tk)
        return step(carry, kl_ref[0, pl.ds(r, tk), :], vl_ref[0, pl.ds(r, tk), :])

    neg = jnp.full((tq, 1), -jnp.inf, F32)
    zero = jnp.zeros((tq, 1), F32)
    carry = (neg, zero, neg, zero, jnp.zeros((tq, 2 * MLA_V_HEAD), F32))
    carry = lax.fori_loop(0, n_lat, lat_step, carry)
    _, l0, _, l1, acc = step(carry, kc_ref[0], vc_ref[0])
    o_ref[0] = (acc / jnp.where(first, l0, l1)).astype(o_ref.dtype)


def _attention(q, k_lat, k_ctx, v_lat, v_ctx, *, tq, tk):
    b, l, _ = q.shape
    c = k_ctx.shape[1]
    pairs = MLA_HEADS // 2
    return pl.pallas_call(
        functools.partial(_attn_kernel, tk=tk),
        grid=(b, pairs, l // tq),
        in_specs=[
            pl.BlockSpec((1, tq, 2 * HEAD_PAD), lambda bi, p, i: (bi, i, p)),
            pl.BlockSpec((1, l, 2 * HEAD_PAD), lambda bi, p, i: (bi, 0, p)),
            pl.BlockSpec((1, c, 2 * HEAD_PAD), lambda bi, p, i: (bi, 0, p)),
            pl.BlockSpec((1, l, 2 * MLA_V_HEAD), lambda bi, p, i: (bi, 0, p)),
            pl.BlockSpec((1, c, 2 * MLA_V_HEAD), lambda bi, p, i: (bi, 0, p)),
        ],
        out_specs=pl.BlockSpec((1, tq, 2 * MLA_V_HEAD), lambda bi, p, i: (bi, i, p)),
        out_shape=jax.ShapeDtypeStruct((b, l, MLA_WIDTH), BF16),
        compiler_params=_params(3),
        name="attn",
    )(q, k_lat, k_ctx, v_lat, v_ctx)


def _gla_kernel(q_ref, k_ref, v_ref, g_ref, s0_ref, o_ref, sout_ref, st_scr, b_scr, *, reverse):
    i = pl.program_id(1)
    t = q_ref.shape[1]
    n_chunks = t // GLA_CHUNK
    grp = min(t, 256)

    @pl.when(i == 0)
    def _():
        st_scr[...] = s0_ref[0]

    r = lax.broadcasted_iota(jnp.int32, (grp, grp), 0)
    c = lax.broadcasted_iota(jnp.int32, (grp, grp), 1)
    same = (r // GLA_CHUNK) == (c // GLA_CHUNK)
    tri = jnp.where(same & ((c >= r) if reverse else (c <= r)), 1.0, 0.0).astype(BF16)
    for gi in range(t // grp):
        g = g_ref[0, gi * grp:(gi + 1) * grp, :]
        hi = g.astype(BF16)
        r1 = g - hi.astype(F32)
        mid = r1.astype(BF16)
        lo = (r1 - mid.astype(F32)).astype(BF16)
        b_scr[gi * grp:(gi + 1) * grp, :] = _dot(tri, hi) + _dot(tri, mid) + _dot(tri, lo)

    rr = lax.broadcasted_iota(jnp.int32, (GLA_CHUNK, GLA_CHUNK), 0)
    cc = lax.broadcasted_iota(jnp.int32, (GLA_CHUNK, GLA_CHUNK), 1)
    keep = (cc >= rr) if reverse else (cc <= rr)
    edge = 0 if reverse else GLA_CHUNK - 1

    def chunk(j, _):
        n = (n_chunks - 1 - j) if reverse else j
        rows = pl.ds(pl.multiple_of(n * GLA_CHUNK, GLA_CHUNK), GLA_CHUNK)
        for hh in range(GLA_HEADS):
            cols = slice(hh * GLA_HEAD_K, (hh + 1) * GLA_HEAD_K)
            bc = b_scr[rows, cols]
            b_all = bc[edge:edge + 1, :]
            qc = q_ref[0, rows, cols].astype(F32)
            kc = k_ref[0, rows, cols].astype(F32)
            vc = v_ref[0, rows, cols]
            qe = (qc * jnp.exp(bc)).astype(BF16)
            ke = (kc * jnp.exp(-bc)).astype(BF16)
            kd = (kc * jnp.exp(b_all - bc)).astype(BF16)
            st = st_scr[hh]
            a = jnp.where(keep, _dot_nt(qe, ke), 0.0).astype(BF16)
            o = _dot(a, vc) + _dot_nt(qe, st.astype(BF16))
            o_ref[0, rows, cols] = o.astype(o_ref.dtype)
            st_scr[hh] = st * jnp.exp(b_all) + _dot_tn(vc, kd)
        return 0

    lax.fori_loop(0, n_chunks, chunk, 0)

    @pl.when(i == pl.num_programs(1) - 1)
    def _():
        sout_ref[0] = st_scr[...]


def _gla(q, k, v, g, s0, *, reverse, tile):
    b, l, _ = q.shape
    tile = min(tile, l)
    nt = l // tile
    ti = (lambda i: nt - 1 - i) if reverse else (lambda i: i)
    gcol = 1 if reverse else 0
    tok = pl.BlockSpec((1, tile, GLA_KEY), lambda bi, i: (bi, ti(i), 0))
    st_spec = pl.BlockSpec((1, GLA_HEADS, GLA_HEAD_V, GLA_HEAD_K), lambda bi, i: (bi, 0, 0, 0))
    return pl.pallas_call(
        functools.partial(_gla_kernel, reverse=reverse),
        grid=(b, nt),
        in_specs=[tok, tok, tok,
                  pl.BlockSpec((1, tile, GLA_KEY), lambda bi, i: (bi, ti(i), gcol)),
                  st_spec],
        out_specs=[tok, st_spec],
        out_shape=[jax.ShapeDtypeStruct((b, l, GLA_VALUE), BF16),
                   jax.ShapeDtypeStruct((b, GLA_HEADS, GLA_HEAD_V, GLA_HEAD_K), F32)],
        scratch_shapes=[pltpu.VMEM((GLA_HEADS, GLA_HEAD_V, GLA_HEAD_K), F32),
                        pltpu.VMEM((tile, GLA_KEY), F32)],
        compiler_params=_params(2),
        name="gla_bwd" if reverse else "gla_fwd",
    )(q, k, v, g, s0)


def _mix_kernel(x_ref, sh_ref, sc_ref, gt_ref, g1_ref, om_ref, of_ref, ob_ref, wb_ref, gng_ref,
                wbm_ref, wbg_ref, wo_ref, o_ref):
    x = x_ref[0]
    h = (_rms(x, g1_ref[...]) * (1.0 + sc_ref[0]) + sh_ref[0]).astype(BF16)
    zr = _dot(h, wb_ref[:, 0:GLA_VALUE])
    og = of_ref[0].astype(F32) + ob_ref[0].astype(F32)
    ys = []
    for hh in range(GLA_HEADS):
        cols = slice(hh * GLA_HEAD_V, (hh + 1) * GLA_HEAD_V)
        r = zr[:, cols]
        ys.append((_rms(og[:, cols], gng_ref[...]) * (r * jax.nn.sigmoid(r))).astype(BF16))
    br_gla = _dot(jnp.concatenate(ys, axis=1), wbg_ref[...])
    br_mla = _dot(om_ref[0], wbm_ref[...])
    g_mla = jax.nn.sigmoid(_dot(h, wb_ref[:, GLA_VALUE:GLA_VALUE + D_MODEL]))
    g_gla = jax.nn.sigmoid(_dot(h, wb_ref[:, GLA_VALUE + D_MODEL:]))
    merged = (g_mla * br_mla + g_gla * br_gla).astype(BF16)
    o_ref[0] = x + gt_ref[0] * _dot(merged, wo_ref[...])


def _mix(x, mod3, g1, o_mla, o_f, o_b, w_b, gng, wbm, wbg, wo, *, tm):
    b, l, _ = x.shape
    tok = lambda w: pl.BlockSpec((1, tm, w), lambda bi, i: (bi, i, 0))
    modc = lambda j: pl.BlockSpec((1, 1, D_MODEL), lambda bi, i: (bi, 0, j))
    return pl.pallas_call(
        _mix_kernel,
        grid=(b, l // tm),
        in_specs=[tok(D_MODEL), modc(0), modc(1), modc(2), _const_spec((1, D_MODEL)),
                  tok(MLA_WIDTH), tok(GLA_VALUE), tok(GLA_VALUE),
                  _const_spec(w_b.shape), _const_spec(gng.shape), _const_spec(wbm.shape),
                  _const_spec(wbg.shape), _const_spec(wo.shape)],
        out_specs=tok(D_MODEL),
        out_shape=jax.ShapeDtypeStruct((b, l, D_MODEL), F32),
        compiler_params=_params(2),
        name="mix",
    )(x, mod3, mod3, mod3, g1, o_mla, o_f, o_b, w_b, gng, wbm, wbg, wo)


FF_CHUNK = 256


def _ffn_kernel(xm_ref, xp_ref, xn_ref, sh_ref, sc_ref, gt_ref, g2_ref, wv_ref, wg_ref, cw_ref, cb_ref,
                wd_ref, fg_ref, o_ref):
    i = pl.program_id(1)
    tm = xm_ref.shape[1]
    te = tm + 2 * GRID_W
    xm = xm_ref[0]
    xe = jnp.concatenate([xp_ref[0], xm, xn_ref[0]], axis=0)
    he = (_rms(xe, g2_ref[...]) * (1.0 + sc_ref[0]) + sh_ref[0]).astype(BF16)
    hm = he[GRID_W:GRID_W + tm]

    row = lax.broadcasted_iota(jnp.int32, (te, 1), 0)
    first_row = jnp.where(i > 0, 0, GRID_W)
    end_row = jnp.where(i < pl.num_programs(1) - 1, te, GRID_W + tm)
    inside = (row >= first_row) & (row < end_row)
    col = row & (GRID_W - 1)
    has_left = col > 0
    has_right = col < GRID_W - 1

    acc = jnp.zeros((tm, D_MODEL), F32)
    for c0 in range(0, D_FF, FF_CHUNK):
        cs = slice(c0, c0 + FF_CHUNK)
        ge = jnp.where(inside, _dot(he, wg_ref[:, cs]), 0.0)
        taps = (jnp.where(has_left, pltpu.roll(ge, 1, 0), 0.0), ge,
                jnp.where(has_right, pltpu.roll(ge, te - 1, 0), 0.0))
        conv = cb_ref[:, cs]
        for dr in range(3):
            rs = slice(dr * GRID_W, dr * GRID_W + tm)
            for dc in range(3):
                conv = conv + taps[dc][rs] * cw_ref[3 * dr + dc:3 * dr + dc + 1, cs]
        gelu = 0.5 * conv * (1.0 + lax.erf(conv * np.float32(2.0 ** -0.5)))
        act = (gelu * _dot(hm, wv_ref[:, cs])).astype(BF16)
        acc = acc + _dot(act, wd_ref[cs, :])
    x2 = xm + gt_ref[0] * acc
    o_ref[0] = _rms(x2, fg_ref[...])


def _ffn(x1, mod3, g2, wv, wg, cw, cb, wd, fg, *, tm):
    b, l, _ = x1.shape
    rpt = tm // GRID_W
    n_rows = l // GRID_W
    tok = lambda w: pl.BlockSpec((1, tm, w), lambda bi, i: (bi, i, 0))
    modc = lambda j: pl.BlockSpec((1, 1, D_MODEL), lambda bi, i: (bi, 0, j))
    return pl.pallas_call(
        _ffn_kernel,
        grid=(b, l // tm),
        in_specs=[tok(D_MODEL),
                  pl.BlockSpec((1, GRID_W, D_MODEL), lambda bi, i: (bi, jnp.maximum(i * rpt - 1, 0), 0)),
                  pl.BlockSpec((1, GRID_W, D_MODEL),
                               lambda bi, i: (bi, jnp.minimum((i + 1) * rpt, n_rows - 1), 0)),
                  modc(3), modc(4), modc(5), _const_spec((1, D_MODEL)),
                  _const_spec(wv.shape), _const_spec(wg.shape), _const_spec(cw.shape), _const_spec(cb.shape),
                  _const_spec(wd.shape), _const_spec((1, D_MODEL))],
        out_specs=tok(D_MODEL),
        out_shape=jax.ShapeDtypeStruct((b, l, D_MODEL), F32),
        compiler_params=_params(2),
        name="ffn",
    )(x1, x1, x1, mod3, mod3, mod3, g2, wv, wg, cw, cb, wd, fg)


def _rope_tables(length):
    t = np.arange(length)
    inv_freq = ROPE_THETA ** (-np.arange(0, ROPE_AXIS_DIM, 2, dtype=np.float32) / ROPE_AXIS_DIM)
    inv_freq = jnp.asarray(inv_freq, F32)
    row = jnp.asarray(t // GRID_W, F32)[:, None] * inv_freq
    col = jnp.asarray(t % GRID_W, F32)[:, None] * inv_freq
    ang = jnp.concatenate([row, row, col, col], axis=1)
    cos, sin = jnp.cos(ang), jnp.sin(ang)
    z8 = jnp.zeros((length, ROPE_AXIS_DIM // 2), F32)
    pad = jnp.zeros((length, HEAD_PAD - MLA_ROPE), F32)
    tq = MLA_SCALE * jnp.concatenate([jnp.ones((length, MLA_NOPE), F32), cos, sin], axis=1)
    s_row, s_col = jnp.sin(row), jnp.sin(col)
    s_next = jnp.concatenate([-s_row, z8, -s_col, z8, pad], axis=1)
    s_prev = jnp.concatenate([z8, s_row, z8, s_col, pad], axis=1)
    return jnp.concatenate([tq, jnp.concatenate([cos, pad], axis=1), s_next, s_prev], axis=1)


def _ctx_tables(length):
    one = jnp.concatenate([jnp.ones((length, MLA_ROPE), F32), jnp.zeros((length, HEAD_PAD - MLA_ROPE), F32)], 1)
    z = jnp.zeros((length, HEAD_PAD), F32)
    return jnp.concatenate([z, one, z, z], axis=1)


def _rot_partner(w):
    h = ROPE_AXIS_DIM // 2
    r1, r2, c1, c2 = w[..., 0:h], w[..., h:2 * h], w[..., 2 * h:3 * h], w[..., 3 * h:4 * h]
    return jnp.concatenate([-r2, r1, -c2, c1], axis=-1)


def _layout_weights(w_in, w_uq, w_ukv, w_decay, b_decay):
    offs = np.cumsum((0, MLA_Q_LORA, MLA_KV_LORA, MLA_ROPE, GLA_KEY, GLA_KEY, GLA_VALUE, GLA_VALUE,
                      2 * GATE_RANK, 2 * D_MODEL))
    part = lambda j: w_in[:, offs[j]:offs[j + 1]]
    zc = lambda n: jnp.zeros((D_MODEL, n), F32)
    w_a = jnp.concatenate([part(0), part(1), part(2), part(7), zc(HEAD_PAD - MLA_ROPE - 2 * GATE_RANK),
                           part(3), part(4), part(5)], axis=1).astype(BF16)
    w_b = jnp.concatenate([part(6), part(8)], axis=1).astype(BF16)

    uq = w_uq.reshape(MLA_Q_LORA, MLA_HEADS, MLA_NOPE + MLA_ROPE)
    rope = uq[..., MLA_NOPE:]
    wuq = jnp.concatenate([uq, _rot_partner(rope)], axis=-1).reshape(MLA_Q_LORA, MLA_HEADS * HEAD_PAD)

    ukv = w_ukv.reshape(MLA_KV_LORA, MLA_HEADS, MLA_NOPE + MLA_V_HEAD)
    k_cols = jnp.concatenate([ukv[..., :MLA_NOPE], jnp.zeros((MLA_KV_LORA, MLA_HEADS, HEAD_PAD - MLA_NOPE), F32)],
                             axis=-1).reshape(MLA_KV_LORA, MLA_HEADS * HEAD_PAD)
    v_cols = ukv[..., MLA_NOPE:].reshape(MLA_KV_LORA, MLA_WIDTH)
    eye = jnp.eye(HEAD_PAD, MLA_ROPE, dtype=F32)
    route = jnp.concatenate([jnp.zeros((HEAD_PAD, MLA_NOPE), F32), eye, eye], axis=1)
    route = jnp.tile(route, (1, MLA_HEADS))
    wkv = jnp.concatenate([jnp.concatenate([k_cols, v_cols], axis=1),
                           jnp.concatenate([route, jnp.zeros((HEAD_PAD, MLA_WIDTH), F32)], axis=1)], axis=0)

    wd = jnp.zeros((HEAD_PAD, 2 * GLA_KEY), F32)
    wd = wd.at[MLA_ROPE:MLA_ROPE + GATE_RANK, :GLA_KEY].set(w_decay[0])
    wd = wd.at[MLA_ROPE + GATE_RANK:MLA_ROPE + 2 * GATE_RANK, GLA_KEY:].set(w_decay[1])
    bd = b_decay.reshape(1, 2 * GLA_KEY)
    return w_a, w_b, wuq.astype(BF16), wkv.astype(BF16), wd.astype(BF16), bd


def kernel(x, c, ctx, c_ctx, w_ada, b_ada, norm1_g, w_in, q_norm_g, w_uq, kv_norm_g, w_ukv, gla_w_decay,
           gla_b_decay, gla_norm_g, w_br_mla, w_br_gla, w_out, norm2_g, w_up, conv_w, conv_b, w_down, final_g):
    b, l, _ = x.shape
    n_ctx = ctx.shape[1]
    assert w_ada.shape[0] == 1, "single layer: context tokens are never updated"
    row2 = lambda a: a.reshape(1, -1)

    c16 = jnp.concatenate([c, c_ctx[None], jnp.zeros((16 - b - 1, D_MODEL), F32)], axis=0)
    mod3 = _ada(c16, w_ada[0], b_ada[0]).reshape(16, 1, 6 * D_MODEL)

    w_a, w_b, wuq, wkv, wd, bd = _layout_weights(w_in[0], w_uq[0], w_ukv[0], gla_w_decay[0], gla_b_decay[0])
    g1 = row2(norm1_g[0])
    proj = functools.partial(_inproj, g1=g1, w_a=w_a, qg=row2(q_norm_g[0]), wuq=wuq, kvg=row2(kv_norm_g[0]),
                             wkv=wkv, wd=wd, bd=bd, tm=512)
    q, k, v, gq, gk, gv, g = proj(x, mod3, None, tab=_rope_tables(l), with_q=True)
    k_c, v_c, gk_c, gv_c, g_c = proj(ctx, mod3, b, tab=_ctx_tables(n_ctx), with_q=False)

    o_mla = _attention(q, k, k_c, v, v_c, tq=512, tk=512)

    s0 = jnp.zeros((b, GLA_HEADS, GLA_HEAD_V, GLA_HEAD_K), F32)
    zq = jnp.zeros((b, n_ctx, GLA_KEY), BF16)
    _, s_f = _gla(zq, gk_c, gv_c, g_c, s0, reverse=False, tile=256)
    _, s_b = _gla(zq, gk_c, gv_c, g_c, s0, reverse=True, tile=256)
    o_f, _ = _gla(gq, gk, gv, g, s_f, reverse=False, tile=512)
    o_b, _ = _gla(gq, gk, gv, g, s_b, reverse=True, tile=512)

    x1 = _mix(x, mod3, g1, o_mla, o_f, o_b, w_b, row2(gla_norm_g[0]), w_br_mla[0].astype(BF16),
              w_br_gla[0].astype(BF16), w_out[0].astype(BF16), tm=512)

    w_up_b = w_up[0].astype(BF16)
    return _ffn(x1, mod3, row2(norm2_g[0]), w_up_b[:, :D_FF], w_up_b[:, D_FF:], conv_w[0].reshape(9, D_FF),
                row2(conv_b[0]), w_down[0].astype(BF16), row2(final_g), tm=512)
```

```python
import functools

import numpy as np
import jax
import jax.numpy as jnp
from jax import lax
from jax.experimental import pallas as pl
from jax.experimental.pallas import tpu as pltpu

F32 = jnp.float32
BF16 = jnp.bfloat16

D_MODEL = 1024
GRID_W = 64
EPS = 1e-6
MLA_HEADS = 8
MLA_NOPE = 64
MLA_ROPE = 32
MLA_V_HEAD = 64
MLA_Q_LORA = 384
MLA_KV_LORA = 256
MLA_WIDTH = MLA_HEADS * MLA_V_HEAD
MLA_SCALE = (MLA_NOPE + MLA_ROPE) ** -0.5
ROPE_AXIS_DIM = MLA_ROPE // 2
ROPE_THETA = 10000.0
GLA_HEADS = 4
GLA_HEAD_K = 128
GLA_HEAD_V = 128
GLA_KEY = GLA_HEADS * GLA_HEAD_K
GLA_VALUE = GLA_HEADS * GLA_HEAD_V
GLA_QSCALE = GLA_HEAD_K ** -0.5
GATE_RANK = 16
GATE_NORM = 16.0
GLA_CHUNK = 64
D_FF = 2816
LOG2E = 1.4426950408889634
HEAD_PAD = 128

VMEM_LIMIT = 56 * 1024 * 1024

A_Q = (0, 384)
A_KV = (384, 640)
A_MISC = (640, 768)
A_GQ = (768, 1280)
A_GK = (1280, 1792)
A_GV = (1792, 2304)
A_COLS = 2304


def _dot(a, b):
    return jnp.dot(a, b, preferred_element_type=F32)


def _dot_nt(a, b):
    return lax.dot_general(a, b, (((1,), (1,)), ((), ())), preferred_element_type=F32)


def _dot_tn(a, b):
    return lax.dot_general(a, b, (((0,), (0,)), ((), ())), preferred_element_type=F32)


def _rms(x, g):
    return x * lax.rsqrt(jnp.mean(x * x, axis=-1, keepdims=True) + EPS) * g


def _const_spec(shape):
    nd = len(shape)
    return pl.BlockSpec(shape, lambda *_: (0,) * nd)


def _params(n_grid):
    return pltpu.CompilerParams(dimension_semantics=("arbitrary",) * n_grid, vmem_limit_bytes=VMEM_LIMIT)


def _ada_kernel(c_ref, w_ref, b_ref, o_ref):
    c = c_ref[...]
    s = (c * jax.nn.sigmoid(c)).astype(BF16)
    o_ref[...] = _dot(s, w_ref[...].astype(BF16)) + b_ref[...]


def _ada(c16, w_ada, b_ada):
    n = w_ada.shape[1]
    bn = 512
    return pl.pallas_call(
        _ada_kernel,
        grid=(n // bn,),
        in_specs=[_const_spec((16, D_MODEL)),
                  pl.BlockSpec((D_MODEL, bn), lambda j: (0, j)),
                  pl.BlockSpec((1, bn), lambda j: (0, j))],
        out_specs=pl.BlockSpec((16, bn), lambda j: (0, j)),
        out_shape=jax.ShapeDtypeStruct((16, n), F32),
        compiler_params=_params(1),
        name="ada",
    )(c16, w_ada, b_ada.reshape(1, n))


def _inproj_kernel(x_ref, sh_ref, sc_ref, g1_ref, tab_ref, w_ref, qg_ref, wuq_ref, kvg_ref, wk_ref, wvt_ref,
                   wd_ref, bd_ref, *out_refs, with_q):
    if with_q:
        q_ref, k_ref, v_ref, gq_ref, gk_ref, gv_ref, g_ref = out_refs
    else:
        k_ref, v_ref, gk_ref, gv_ref, g_ref = out_refs
    x = x_ref[0]
    h = (_rms(x, g1_ref[...]) * (1.0 + sc_ref[0]) + sh_ref[0]).astype(BF16)

    if with_q:
        zq = _dot(h, w_ref[:, A_Q[0]:A_Q[1]])
        qn = _rms(zq, qg_ref[...]).astype(BF16)
        qf = _dot(qn, wuq_ref[...])
        tq = tab_ref[:, 0:HEAD_PAD]
        for hh in range(MLA_HEADS):
            sl = slice(hh * HEAD_PAD, (hh + 1) * HEAD_PAD)
            q_ref[0, :, sl] = (qf[:, sl] * tq).astype(BF16)
        gq_ref[0] = (_dot(h, w_ref[:, A_GQ[0]:A_GQ[1]]) * GLA_QSCALE).astype(BF16)

    zkv = _dot(h, w_ref[:, A_KV[0]:A_KV[1]])
    kvn = _rms(zkv, kvg_ref[...]).astype(BF16)
    misc = _dot(h, w_ref[:, A_MISC[0]:A_MISC[1]])
    kr = (misc * tab_ref[:, 128:256]
          + pltpu.roll(misc, HEAD_PAD - 8, 1) * tab_ref[:, 256:384]
          + pltpu.roll(misc, 8, 1) * tab_ref[:, 384:512])
    k_ref[0] = _dot(jnp.concatenate([kvn, kr.astype(BF16)], axis=1), wk_ref[...]).astype(BF16)
    v_ref[0, 0] = _dot_nt(wvt_ref[...], kvn).astype(BF16)

    gk_ref[0] = _dot(h, w_ref[:, A_GK[0]:A_GK[1]]).astype(BF16)
    gv_ref[0] = _dot(h, w_ref[:, A_GV[0]:A_GV[1]]).astype(BF16)

    gp = _dot(misc.astype(BF16), wd_ref[...]) + bd_ref[...]
    g_ref[0] = (jnp.minimum(gp, 0.0) - jnp.log1p(jnp.exp(-jnp.abs(gp)))) * (1.0 / GATE_NORM)


def _inproj(x, mod3, mod_row, g1, tab, w_a, qg, wuq, kvg, wk, wvt, wd, bd, *, tm, with_q):
    b, l, _ = x.shape
    tm = min(tm, l)
    nt = l // tm
    row = (lambda bi: bi) if mod_row is None else (lambda bi: mod_row)
    tok = lambda w: pl.BlockSpec((1, tm, w), lambda bi, i: (bi, i, 0))
    in_specs = [
        tok(D_MODEL),
        pl.BlockSpec((1, 1, D_MODEL), lambda bi, i: (row(bi), 0, 0)),
        pl.BlockSpec((1, 1, D_MODEL), lambda bi, i: (row(bi), 0, 1)),
        _const_spec((1, D_MODEL)),
        pl.BlockSpec((tm, 512), lambda bi, i: (i, 0)),
        _const_spec(w_a.shape), _const_spec(qg.shape), _const_spec(wuq.shape), _const_spec(kvg.shape),
        _const_spec(wk.shape), _const_spec(wvt.shape), _const_spec(wd.shape), _const_spec(bd.shape),
    ]
    sds = lambda w, dt: jax.ShapeDtypeStruct((b, l, w), dt)
    kw = MLA_HEADS * HEAD_PAD
    outs = [(kw, BF16), None, (GLA_KEY, BF16), (GLA_VALUE, BF16), (2 * GLA_KEY, F32)]
    if with_q:
        outs = [(kw, BF16)] + outs[:2] + [(GLA_KEY, BF16)] + outs[2:]
    vt_spec = pl.BlockSpec((1, 1, MLA_WIDTH, tm), lambda bi, i: (bi, i, 0, 0))
    vt_shape = jax.ShapeDtypeStruct((b, nt, MLA_WIDTH, tm), BF16)
    return pl.pallas_call(
        functools.partial(_inproj_kernel, with_q=with_q),
        grid=(b, nt),
        in_specs=in_specs,
        out_specs=[vt_spec if o is None else tok(o[0]) for o in outs],
        out_shape=[vt_shape if o is None else sds(*o) for o in outs],
        compiler_params=_params(2),
        name="inproj_lat" if with_q else "inproj_ctx",
    )(x, mod3, mod3, g1, tab, w_a, qg, wuq, kvg, wk, wvt, wd, bd)


def _attn_kernel(q_ref, kl_ref, kc_ref, vl_ref, vc_ref, o_ref, sa_ref, sb_ref, sc_ref):
    tq = q_ref.shape[1]
    n_lat, _, tk = vl_ref.shape[1:]
    assert n_lat >= 2 and n_lat % 2 == 0

    def scores(dst, k_t):
        for hh in range(2):
            cols = slice(hh * HEAD_PAD, (hh + 1) * HEAD_PAD)
            dst[hh] = _dot_nt(k_t[:, cols], q_ref[0, :, cols])

    def consume(src, vt, carry):
        out = []
        for hh in range(2):
            m, l, acc = carry[hh]
            s = src[hh]
            m_new = jnp.maximum(m, jnp.max(s, axis=0, keepdims=True))
            a = jnp.exp2(m - m_new)
            p = jnp.exp2(s - m_new)
            l_new = a * l + jnp.sum(p, axis=0, keepdims=True)
            pv = _dot(vt[hh * MLA_V_HEAD:(hh + 1) * MLA_V_HEAD], p.astype(BF16))
            out.append((m_new, l_new, acc * a + pv))
        return tuple(out)

    def k_tile(i):
        return kl_ref[0, pl.ds(pl.multiple_of(i * tk, tk), tk), :]

    def pair(it, carry):
        j = 2 * it
        scores(sb_ref, k_tile(j + 1))
        carry = consume(sa_ref, vl_ref[0, j], carry)
        scores(sa_ref, k_tile(j + 2))
        return consume(sb_ref, vl_ref[0, j + 1], carry)

    head0 = (jnp.full((1, tq), -jnp.inf, F32), jnp.zeros((1, tq), F32), jnp.zeros((MLA_V_HEAD, tq), F32))
    scores(sa_ref, k_tile(0))
    carry = lax.fori_loop(0, (n_lat - 2) // 2, pair, (head0, head0))
    scores(sb_ref, k_tile(n_lat - 1))
    carry = consume(sa_ref, vl_ref[0, n_lat - 2], carry)
    scores(sc_ref, kc_ref[0])
    carry = consume(sb_ref, vl_ref[0, n_lat - 1], carry)
    (_, l0, acc0), (_, l1, acc1) = consume(sc_ref, vc_ref[0, 0], carry)
    ot = jnp.concatenate([acc0 / l0, acc1 / l1], axis=0)
    o_ref[0] = ot.T.astype(o_ref.dtype)


def _attention(q, k_lat, k_ctx, vt_lat, vt_ctx, *, tq):
    b, l, _ = q.shape
    c = k_ctx.shape[1]
    n_lat, _, tk = vt_lat.shape[1:]
    pairs = MLA_HEADS // 2
    return pl.pallas_call(
        _attn_kernel,
        grid=(b, pairs, l // tq),
        in_specs=[
            pl.BlockSpec((1, tq, 2 * HEAD_PAD), lambda bi, p, i: (bi, i, p)),
            pl.BlockSpec((1, l, 2 * HEAD_PAD), lambda bi, p, i: (bi, 0, p)),
            pl.BlockSpec((1, c, 2 * HEAD_PAD), lambda bi, p, i: (bi, 0, p)),
            pl.BlockSpec((1, n_lat, 2 * MLA_V_HEAD, tk), lambda bi, p, i: (bi, 0, p, 0)),
            pl.BlockSpec((1, 1, 2 * MLA_V_HEAD, c), lambda bi, p, i: (bi, 0, p, 0)),
        ],
        out_specs=pl.BlockSpec((1, tq, 2 * MLA_V_HEAD), lambda bi, p, i: (bi, i, p)),
        out_shape=jax.ShapeDtypeStruct((b, l, MLA_WIDTH), BF16),
        scratch_shapes=[pltpu.VMEM((2, tk, tq), F32), pltpu.VMEM((2, tk, tq), F32), pltpu.VMEM((2, c, tq), F32)],
        compiler_params=_params(3),
        name="attn",
    )(q, k_lat, k_ctx, vt_lat, vt_ctx)


def _gla_kernel(q_ref, k_ref, v_ref, g_ref, s0_ref, o_ref, sout_ref, st_scr, b_scr, *, reverse):
    i = pl.program_id(1)
    t = q_ref.shape[1]
    n_chunks = t // GLA_CHUNK
    grp = min(t, 256)

    @pl.when(i == 0)
    def _():
        st_scr[...] = s0_ref[0]

    r = lax.broadcasted_iota(jnp.int32, (grp, grp), 0)
    c = lax.broadcasted_iota(jnp.int32, (grp, grp), 1)
    same = (r // GLA_CHUNK) == (c // GLA_CHUNK)
    tri = jnp.where(same & ((c >= r) if reverse else (c <= r)), 1.0, 0.0).astype(BF16)
    for gi in range(t // grp):
        g = g_ref[0, gi * grp:(gi + 1) * grp, :]
        hi = g.astype(BF16)
        r1 = g - hi.astype(F32)
        mid = r1.astype(BF16)
        lo = (r1 - mid.astype(F32)).astype(BF16)
        b_scr[gi * grp:(gi + 1) * grp, :] = _dot(tri, hi) + _dot(tri, mid) + _dot(tri, lo)

    rr = lax.broadcasted_iota(jnp.int32, (GLA_CHUNK, GLA_CHUNK), 0)
    cc = lax.broadcasted_iota(jnp.int32, (GLA_CHUNK, GLA_CHUNK), 1)
    keep = (cc >= rr) if reverse else (cc <= rr)
    edge = 0 if reverse else GLA_CHUNK - 1

    def chunk(j, _):
        n = (n_chunks - 1 - j) if reverse else j
        rows = pl.ds(pl.multiple_of(n * GLA_CHUNK, GLA_CHUNK), GLA_CHUNK)
        for hh in range(GLA_HEADS):
            cols = slice(hh * GLA_HEAD_K, (hh + 1) * GLA_HEAD_K)
            bc = b_scr[rows, cols]
            b_all = bc[edge:edge + 1, :]
            qc = q_ref[0, rows, cols].astype(F32)
            kc = k_ref[0, rows, cols].astype(F32)
            vc = v_ref[0, rows, cols]
            qe = (qc * jnp.exp(bc)).astype(BF16)
            ke = (kc * jnp.exp(-bc)).astype(BF16)
            kd = (kc * jnp.exp(b_all - bc)).astype(BF16)
            st = st_scr[hh]
            a = jnp.where(keep, _dot_nt(qe, ke), 0.0).astype(BF16)
            o = _dot(a, vc) + _dot_nt(qe, st.astype(BF16))
            o_ref[0, rows, cols] = o.astype(o_ref.dtype)
            st_scr[hh] = st * jnp.exp(b_all) + _dot_tn(vc, kd)
        return 0

    lax.fori_loop(0, n_chunks, chunk, 0)

    @pl.when(i == pl.num_programs(1) - 1)
    def _():
        sout_ref[0] = st_scr[...]


def _gla(q, k, v, g, s0, *, reverse, tile):
    b, l, _ = q.shape
    tile = min(tile, l)
    nt = l // tile
    ti = (lambda i: nt - 1 - i) if reverse else (lambda i: i)
    gcol = 1 if reverse else 0
    tok = pl.BlockSpec((1, tile, GLA_KEY), lambda bi, i: (bi, ti(i), 0))
    st_spec = pl.BlockSpec((1, GLA_HEADS, GLA_HEAD_V, GLA_HEAD_K), lambda bi, i: (bi, 0, 0, 0))
    return pl.pallas_call(
        functools.partial(_gla_kernel, reverse=reverse),
        grid=(b, nt),
        in_specs=[tok, tok, tok,
                  pl.BlockSpec((1, tile, GLA_KEY), lambda bi, i: (bi, ti(i), gcol)),
                  st_spec],
        out_specs=[tok, st_spec],
        out_shape=[jax.ShapeDtypeStruct((b, l, GLA_VALUE), BF16),
                   jax.ShapeDtypeStruct((b, GLA_HEADS, GLA_HEAD_V, GLA_HEAD_K), F32)],
        scratch_shapes=[pltpu.VMEM((GLA_HEADS, GLA_HEAD_V, GLA_HEAD_K), F32),
                        pltpu.VMEM((tile, GLA_KEY), F32)],
        compiler_params=_params(2),
        name="gla_bwd" if reverse else "gla_fwd",
    )(q, k, v, g, s0)


def _mix_kernel(x_ref, sh_ref, sc_ref, gt_ref, g1_ref, om_ref, of_ref, ob_ref, wb_ref, gng_ref,
                wbm_ref, wbg_ref, wo_ref, o_ref):
    x = x_ref[0]
    h = (_rms(x, g1_ref[...]) * (1.0 + sc_ref[0]) + sh_ref[0]).astype(BF16)
    zr = _dot(h, wb_ref[:, 0:GLA_VALUE])
    og = of_ref[0].astype(F32) + ob_ref[0].astype(F32)
    ys = []
    for hh in range(GLA_HEADS):
        cols = slice(hh * GLA_HEAD_V, (hh + 1) * GLA_HEAD_V)
        r = zr[:, cols]
        ys.append((_rms(og[:, cols], gng_ref[...]) * (r * jax.nn.sigmoid(r))).astype(BF16))
    br_gla = _dot(jnp.concatenate(ys, axis=1), wbg_ref[...])
    br_mla = _dot(om_ref[0], wbm_ref[...])
    g_mla = jax.nn.sigmoid(_dot(h, wb_ref[:, GLA_VALUE:GLA_VALUE + D_MODEL]))
    g_gla = jax.nn.sigmoid(_dot(h, wb_ref[:, GLA_VALUE + D_MODEL:]))
    merged = (g_mla * br_mla + g_gla * br_gla).astype(BF16)
    o_ref[0] = x + gt_ref[0] * _dot(merged, wo_ref[...])


def _mix(x, mod3, g1, o_mla, o_f, o_b, w_b, gng, wbm, wbg, wo, *, tm):
    b, l, _ = x.shape
    tok = lambda w: pl.BlockSpec((1, tm, w), lambda bi, i: (bi, i, 0))
    modc = lambda j: pl.BlockSpec((1, 1, D_MODEL), lambda bi, i: (bi, 0, j))
    return pl.pallas_call(
        _mix_kernel,
        grid=(b, l // tm),
        in_specs=[tok(D_MODEL), modc(0), modc(1), modc(2), _const_spec((1, D_MODEL)),
                  tok(MLA_WIDTH), tok(GLA_VALUE), tok(GLA_VALUE),
                  _const_spec(w_b.shape), _const_spec(gng.shape), _const_spec(wbm.shape),
                  _const_spec(wbg.shape), _const_spec(wo.shape)],
        out_specs=tok(D_MODEL),
        out_shape=jax.ShapeDtypeStruct((b, l, D_MODEL), F32),
        compiler_params=_params(2),
        name="mix",
    )(x, mod3, mod3, mod3, g1, o_mla, o_f, o_b, w_b, gng, wbm, wbg, wo)


FF_CHUNK = 256


def _ffn_kernel(xm_ref, xp_ref, xn_ref, sh_ref, sc_ref, gt_ref, g2_ref, wv_ref, wg_ref, cw_ref, cb_ref,
                wd_ref, fg_ref, o_ref):
    i = pl.program_id(1)
    tm = xm_ref.shape[1]
    te = tm + 2 * GRID_W
    xm = xm_ref[0]
    xe = jnp.concatenate([xp_ref[0], xm, xn_ref[0]], axis=0)
    he = (_rms(xe, g2_ref[...]) * (1.0 + sc_ref[0]) + sh_ref[0]).astype(BF16)
    hm = he[GRID_W:GRID_W + tm]

    row = lax.broadcasted_iota(jnp.int32, (te, 1), 0)
    first_row = jnp.where(i > 0, 0, GRID_W)
    end_row = jnp.where(i < pl.num_programs(1) - 1, te, GRID_W + tm)
    inside = (row >= first_row) & (row < end_row)
    col = row & (GRID_W - 1)
    has_left = col > 0
    has_right = col < GRID_W - 1

    acc = jnp.zeros((tm, D_MODEL), F32)
    for c0 in range(0, D_FF, FF_CHUNK):
        cs = slice(c0, c0 + FF_CHUNK)
        ge = jnp.where(inside, _dot(he, wg_ref[:, cs]), 0.0)
        taps = (jnp.where(has_left, pltpu.roll(ge, 1, 0), 0.0), ge,
                jnp.where(has_right, pltpu.roll(ge, te - 1, 0), 0.0))
        conv = cb_ref[:, cs]
        for dr in range(3):
            rs = slice(dr * GRID_W, dr * GRID_W + tm)
            for dc in range(3):
                conv = conv + taps[dc][rs] * cw_ref[3 * dr + dc:3 * dr + dc + 1, cs]
        gelu = 0.5 * conv * (1.0 + lax.erf(conv * np.float32(2.0 ** -0.5)))
        act = (gelu * _dot(hm, wv_ref[:, cs])).astype(BF16)
        acc = acc + _dot(act, wd_ref[cs, :])
    x2 = xm + gt_ref[0] * acc
    o_ref[0] = _rms(x2, fg_ref[...])


def _ffn(x1, mod3, g2, wv, wg, cw, cb, wd, fg, *, tm):
    b, l, _ = x1.shape
    rpt = tm // GRID_W
    n_rows = l // GRID_W
    tok = lambda w: pl.BlockSpec((1, tm, w), lambda bi, i: (bi, i, 0))
    modc = lambda j: pl.BlockSpec((1, 1, D_MODEL), lambda bi, i: (bi, 0, j))
    return pl.pallas_call(
        _ffn_kernel,
        grid=(b, l // tm),
        in_specs=[tok(D_MODEL),
                  pl.BlockSpec((1, GRID_W, D_MODEL), lambda bi, i: (bi, jnp.maximum(i * rpt - 1, 0), 0)),
                  pl.BlockSpec((1, GRID_W, D_MODEL),
                               lambda bi, i: (bi, jnp.minimum((i + 1) * rpt, n_rows - 1), 0)),
                  modc(3), modc(4), modc(5), _const_spec((1, D_MODEL)),
                  _const_spec(wv.shape), _const_spec(wg.shape), _const_spec(cw.shape), _const_spec(cb.shape),
                  _const_spec(wd.shape), _const_spec((1, D_MODEL))],
        out_specs=tok(D_MODEL),
        out_shape=jax.ShapeDtypeStruct((b, l, D_MODEL), F32),
        compiler_params=_params(2),
        name="ffn",
    )(x1, x1, x1, mod3, mod3, mod3, g2, wv, wg, cw, cb, wd, fg)


def _rope_tables(length):
    t = np.arange(length)
    inv_freq = ROPE_THETA ** (-np.arange(0, ROPE_AXIS_DIM, 2, dtype=np.float32) / ROPE_AXIS_DIM)
    inv_freq = jnp.asarray(inv_freq, F32)
    row = jnp.asarray(t // GRID_W, F32)[:, None] * inv_freq
    col = jnp.asarray(t % GRID_W, F32)[:, None] * inv_freq
    ang = jnp.concatenate([row, row, col, col], axis=1)
    cos, sin = jnp.cos(ang), jnp.sin(ang)
    z8 = jnp.zeros((length, ROPE_AXIS_DIM // 2), F32)
    pad = jnp.zeros((length, HEAD_PAD - MLA_ROPE), F32)
    tq = (MLA_SCALE * LOG2E) * jnp.concatenate([jnp.ones((length, MLA_NOPE), F32), cos, sin], axis=1)
    s_row, s_col = jnp.sin(row), jnp.sin(col)
    s_next = jnp.concatenate([-s_row, z8, -s_col, z8, pad], axis=1)
    s_prev = jnp.concatenate([z8, s_row, z8, s_col, pad], axis=1)
    return jnp.concatenate([tq, jnp.concatenate([cos, pad], axis=1), s_next, s_prev], axis=1)


def _ctx_tables(length):
    one = jnp.concatenate([jnp.ones((length, MLA_ROPE), F32), jnp.zeros((length, HEAD_PAD - MLA_ROPE), F32)], 1)
    z = jnp.zeros((length, HEAD_PAD), F32)
    return jnp.concatenate([z, one, z, z], axis=1)


def _rot_partner(w):
    h = ROPE_AXIS_DIM // 2
    r1, r2, c1, c2 = w[..., 0:h], w[..., h:2 * h], w[..., 2 * h:3 * h], w[..., 3 * h:4 * h]
    return jnp.concatenate([-r2, r1, -c2, c1], axis=-1)


def _layout_weights(w_in, w_uq, w_ukv, w_decay, b_decay):
    offs = np.cumsum((0, MLA_Q_LORA, MLA_KV_LORA, MLA_ROPE, GLA_KEY, GLA_KEY, GLA_VALUE, GLA_VALUE,
                      2 * GATE_RANK, 2 * D_MODEL))
    part = lambda j: w_in[:, offs[j]:offs[j + 1]]
    zc = lambda n: jnp.zeros((D_MODEL, n), F32)
    w_a = jnp.concatenate([part(0), part(1), part(2), part(7), zc(HEAD_PAD - MLA_ROPE - 2 * GATE_RANK),
                           part(3), part(4), part(5)], axis=1).astype(BF16)
    w_b = jnp.concatenate([part(6), part(8)], axis=1).astype(BF16)

    uq = w_uq.reshape(MLA_Q_LORA, MLA_HEADS, MLA_NOPE + MLA_ROPE)
    rope = uq[..., MLA_NOPE:]
    wuq = jnp.concatenate([uq, _rot_partner(rope)], axis=-1).reshape(MLA_Q_LORA, MLA_HEADS * HEAD_PAD)

    ukv = w_ukv.reshape(MLA_KV_LORA, MLA_HEADS, MLA_NOPE + MLA_V_HEAD)
    k_cols = jnp.concatenate([ukv[..., :MLA_NOPE], jnp.zeros((MLA_KV_LORA, MLA_HEADS, HEAD_PAD - MLA_NOPE), F32)],
                             axis=-1).reshape(MLA_KV_LORA, MLA_HEADS * HEAD_PAD)
    v_cols = ukv[..., MLA_NOPE:].reshape(MLA_KV_LORA, MLA_WIDTH)
    eye = jnp.eye(HEAD_PAD, MLA_ROPE, dtype=F32)
    route = jnp.concatenate([jnp.zeros((HEAD_PAD, MLA_NOPE), F32), eye, eye], axis=1)
    route = jnp.tile(route, (1, MLA_HEADS))
    wk = jnp.concatenate([k_cols, route], axis=0)

    wd = jnp.zeros((HEAD_PAD, 2 * GLA_KEY), F32)
    wd = wd.at[MLA_ROPE:MLA_ROPE + GATE_RANK, :GLA_KEY].set(w_decay[0])
    wd = wd.at[MLA_ROPE + GATE_RANK:MLA_ROPE + 2 * GATE_RANK, GLA_KEY:].set(w_decay[1])
    bd = b_decay.reshape(1, 2 * GLA_KEY)
    return w_a, w_b, wuq.astype(BF16), wk.astype(BF16), v_cols.T.astype(BF16), wd.astype(BF16), bd


def kernel(x, c, ctx, c_ctx, w_ada, b_ada, norm1_g, w_in, q_norm_g, w_uq, kv_norm_g, w_ukv, gla_w_decay,
           gla_b_decay, gla_norm_g, w_br_mla, w_br_gla, w_out, norm2_g, w_up, conv_w, conv_b, w_down, final_g):
    b, l, _ = x.shape
    n_ctx = ctx.shape[1]
    assert w_ada.shape[0] == 1, "single layer: context tokens are never updated"
    row2 = lambda a: a.reshape(1, -1)

    c16 = jnp.concatenate([c, c_ctx[None], jnp.zeros((16 - b - 1, D_MODEL), F32)], axis=0)
    mod3 = _ada(c16, w_ada[0], b_ada[0]).reshape(16, 1, 6 * D_MODEL)

    w_a, w_b, wuq, wk, wvt, wd, bd = _layout_weights(w_in[0], w_uq[0], w_ukv[0], gla_w_decay[0], gla_b_decay[0])
    g1 = row2(norm1_g[0])
    proj = functools.partial(_inproj, g1=g1, w_a=w_a, qg=row2(q_norm_g[0]), wuq=wuq, kvg=row2(kv_norm_g[0]),
                             wk=wk, wvt=wvt, wd=wd, bd=bd, tm=512)
    q, k, v, gq, gk, gv, g = proj(x, mod3, None, tab=_rope_tables(l), with_q=True)
    k_c, v_c, gk_c, gv_c, g_c = proj(ctx, mod3, b, tab=_ctx_tables(n_ctx), with_q=False)

    o_mla = _attention(q, k, k_c, v, v_c, tq=512)

    s0 = jnp.zeros((b, GLA_HEADS, GLA_HEAD_V, GLA_HEAD_K), F32)
    zq = jnp.zeros((b, n_ctx, GLA_KEY), BF16)
    _, s_f = _gla(zq, gk_c, gv_c, g_c, s0, reverse=False, tile=256)
    _, s_b = _gla(zq, gk_c, gv_c, g_c, s0, reverse=True, tile=256)
    o_f, _ = _gla(gq, gk, gv, g, s_f, reverse=False, tile=512)
    o_b, _ = _gla(gq, gk, gv, g, s_b, reverse=True, tile=512)

    x1 = _mix(x, mod3, g1, o_mla, o_f, o_b, w_b, row2(gla_norm_g[0]), w_br_mla[0].astype(BF16),
              w_br_gla[0].astype(BF16), w_out[0].astype(BF16), tm=512)

    w_up_b = w_up[0].astype(BF16)
    return _ffn(x1, mod3, row2(norm2_g[0]), w_up_b[:, :D_FF], w_up_b[:, D_FF:], conv_w[0].reshape(9, D_FF),
                row2(conv_b[0]), w_down[0].astype(BF16), row2(final_g), tm=512)
```

```python
import functools

import numpy as np
import jax
import jax.numpy as jnp
from jax import lax
from jax.experimental import pallas as pl
from jax.experimental.pallas import tpu as pltpu

F32 = jnp.float32
BF16 = jnp.bfloat16

D_MODEL = 1024
GRID_W = 64
EPS = 1e-6
MLA_HEADS = 8
MLA_NOPE = 64
MLA_ROPE = 32
MLA_V_HEAD = 64
MLA_Q_LORA = 384
MLA_KV_LORA = 256
MLA_WIDTH = MLA_HEADS * MLA_V_HEAD
MLA_SCALE = (MLA_NOPE + MLA_ROPE) ** -0.5
ROPE_AXIS_DIM = MLA_ROPE // 2
ROPE_THETA = 10000.0
GLA_HEADS = 4
GLA_HEAD_K = 128
GLA_HEAD_V = 128
GLA_KEY = GLA_HEADS * GLA_HEAD_K
GLA_VALUE = GLA_HEADS * GLA_HEAD_V
GLA_QSCALE = GLA_HEAD_K ** -0.5
GATE_RANK = 16
GATE_NORM = 16.0
GLA_CHUNK = 64
D_FF = 2816
LOG2E = 1.4426950408889634
ONES_ROWS = 16
HEAD_PAD = 128

VMEM_LIMIT = 56 * 1024 * 1024

A_Q = (0, 384)
A_KV = (384, 640)
A_MISC = (640, 768)
A_GQ = (768, 1280)
A_GK = (1280, 1792)
A_GV = (1792, 2304)
A_COLS = 2304


def _dot(a, b):
    return jnp.dot(a, b, preferred_element_type=F32)


def _dot_nt(a, b):
    return lax.dot_general(a, b, (((1,), (1,)), ((), ())), preferred_element_type=F32)


def _dot_tn(a, b):
    return lax.dot_general(a, b, (((0,), (0,)), ((), ())), preferred_element_type=F32)


def _rms(x, g):
    return x * lax.rsqrt(jnp.mean(x * x, axis=-1, keepdims=True) + EPS) * g


def _const_spec(shape):
    nd = len(shape)
    return pl.BlockSpec(shape, lambda *_: (0,) * nd)


def _params(n_grid):
    return pltpu.CompilerParams(dimension_semantics=("arbitrary",) * n_grid, vmem_limit_bytes=VMEM_LIMIT)


def _ada_kernel(c_ref, w_ref, b_ref, o_ref):
    c = c_ref[...]
    s = (c * jax.nn.sigmoid(c)).astype(BF16)
    o_ref[...] = _dot(s, w_ref[...].astype(BF16)) + b_ref[...]


def _ada(c16, w_ada, b_ada):
    n = w_ada.shape[1]
    bn = 512
    return pl.pallas_call(
        _ada_kernel,
        grid=(n // bn,),
        in_specs=[_const_spec((16, D_MODEL)),
                  pl.BlockSpec((D_MODEL, bn), lambda j: (0, j)),
                  pl.BlockSpec((1, bn), lambda j: (0, j))],
        out_specs=pl.BlockSpec((16, bn), lambda j: (0, j)),
        out_shape=jax.ShapeDtypeStruct((16, n), F32),
        compiler_params=_params(1),
        name="ada",
    )(c16, w_ada, b_ada.reshape(1, n))


def _inproj_kernel(x_ref, sh_ref, sc_ref, g1_ref, tab_ref, w_ref, qg_ref, wuq_ref, kvg_ref, wk_ref, wvt_ref,
                   wd_ref, bd_ref, *out_refs, with_q):
    if with_q:
        q_ref, k_ref, v_ref, gq_ref, gk_ref, gv_ref, g_ref = out_refs
    else:
        k_ref, v_ref, gk_ref, gv_ref, g_ref = out_refs
    x = x_ref[0]
    h = (_rms(x, g1_ref[...]) * (1.0 + sc_ref[0]) + sh_ref[0]).astype(BF16)

    if with_q:
        zq = _dot(h, w_ref[:, A_Q[0]:A_Q[1]])
        qn = _rms(zq, qg_ref[...]).astype(BF16)
        qf = _dot(qn, wuq_ref[...])
        tq = tab_ref[:, 0:HEAD_PAD]
        for hh in range(MLA_HEADS):
            sl = slice(hh * HEAD_PAD, (hh + 1) * HEAD_PAD)
            q_ref[0, :, sl] = (qf[:, sl] * tq).astype(BF16)
        gq_ref[0] = (_dot(h, w_ref[:, A_GQ[0]:A_GQ[1]]) * GLA_QSCALE).astype(BF16)

    zkv = _dot(h, w_ref[:, A_KV[0]:A_KV[1]])
    kvn = _rms(zkv, kvg_ref[...]).astype(BF16)
    misc = _dot(h, w_ref[:, A_MISC[0]:A_MISC[1]])
    kr = (misc * tab_ref[:, 128:256]
          + pltpu.roll(misc, HEAD_PAD - 8, 1) * tab_ref[:, 256:384]
          + pltpu.roll(misc, 8, 1) * tab_ref[:, 384:512])
    k_ref[0] = _dot(jnp.concatenate([kvn, kr.astype(BF16)], axis=1), wk_ref[...]).astype(BF16)
    v_ref[0, 0] = _dot_nt(wvt_ref[...], kvn).astype(BF16)

    gk_ref[0] = _dot(h, w_ref[:, A_GK[0]:A_GK[1]]).astype(BF16)
    gv_ref[0] = _dot(h, w_ref[:, A_GV[0]:A_GV[1]]).astype(BF16)

    gp = _dot(misc.astype(BF16), wd_ref[...]) + bd_ref[...]
    g_ref[0] = (jnp.minimum(gp, 0.0) - jnp.log1p(jnp.exp(-jnp.abs(gp)))) * (1.0 / GATE_NORM)


def _inproj(x, mod3, mod_row, g1, tab, w_a, qg, wuq, kvg, wk, wvt, wd, bd, *, tm, with_q):
    b, l, _ = x.shape
    tm = min(tm, l)
    nt = l // tm
    row = (lambda bi: bi) if mod_row is None else (lambda bi: mod_row)
    tok = lambda w: pl.BlockSpec((1, tm, w), lambda bi, i: (bi, i, 0))
    in_specs = [
        tok(D_MODEL),
        pl.BlockSpec((1, 1, D_MODEL), lambda bi, i: (row(bi), 0, 0)),
        pl.BlockSpec((1, 1, D_MODEL), lambda bi, i: (row(bi), 0, 1)),
        _const_spec((1, D_MODEL)),
        pl.BlockSpec((tm, 512), lambda bi, i: (i, 0)),
        _const_spec(w_a.shape), _const_spec(qg.shape), _const_spec(wuq.shape), _const_spec(kvg.shape),
        _const_spec(wk.shape), _const_spec(wvt.shape), _const_spec(wd.shape), _const_spec(bd.shape),
    ]
    sds = lambda w, dt: jax.ShapeDtypeStruct((b, l, w), dt)
    kw = MLA_HEADS * HEAD_PAD
    outs = [(kw, BF16), None, (GLA_KEY, BF16), (GLA_VALUE, BF16), (2 * GLA_KEY, F32)]
    if with_q:
        outs = [(kw, BF16)] + outs[:2] + [(GLA_KEY, BF16)] + outs[2:]
    vt_spec = pl.BlockSpec((1, 1, MLA_WIDTH, tm), lambda bi, i: (bi, i, 0, 0))
    vt_shape = jax.ShapeDtypeStruct((b, nt, MLA_WIDTH, tm), BF16)
    return pl.pallas_call(
        functools.partial(_inproj_kernel, with_q=with_q),
        grid=(b, nt),
        in_specs=in_specs,
        out_specs=[vt_spec if o is None else tok(o[0]) for o in outs],
        out_shape=[vt_shape if o is None else sds(*o) for o in outs],
        compiler_params=_params(2),
        name="inproj_lat" if with_q else "inproj_ctx",
    )(x, mod3, mod3, g1, tab, w_a, qg, wuq, kvg, wk, wvt, wd, bd)


def _attn_kernel(q_ref, kl_ref, kc_ref, vl_ref, vc_ref, o_ref, sa_ref, sb_ref, sc_ref):
    tq = q_ref.shape[1]
    n_lat, _, tk = vl_ref.shape[1:]
    assert n_lat >= 2 and n_lat % 2 == 0

    def scores(dst, k_t):
        for hh in range(2):
            cols = slice(hh * HEAD_PAD, (hh + 1) * HEAD_PAD)
            dst[hh] = _dot_nt(k_t[:, cols], q_ref[0, :, cols])

    def consume(src, vt, carry):
        ones = jnp.ones((ONES_ROWS, vt.shape[1]), BF16)
        out = []
        for hh in range(2):
            m, acc = carry[hh]
            s = src[hh]
            m_new = jnp.maximum(m, jnp.max(s, axis=0, keepdims=True))
            p = jnp.exp2(s - m_new).astype(BF16)
            v_aug = jnp.concatenate([vt[hh * MLA_V_HEAD:(hh + 1) * MLA_V_HEAD], ones], axis=0)
            out.append((m_new, acc * jnp.exp2(m - m_new) + _dot(v_aug, p)))
        return tuple(out)

    def k_tile(i):
        return kl_ref[0, pl.ds(pl.multiple_of(i * tk, tk), tk), :]

    def pair(it, carry):
        j = 2 * it
        scores(sb_ref, k_tile(j + 1))
        carry = consume(sa_ref, vl_ref[0, j], carry)
        scores(sa_ref, k_tile(j + 2))
        return consume(sb_ref, vl_ref[0, j + 1], carry)

    head0 = (jnp.full((1, tq), -jnp.inf, F32), jnp.zeros((MLA_V_HEAD + ONES_ROWS, tq), F32))
    scores(sa_ref, k_tile(0))
    carry = lax.fori_loop(0, (n_lat - 2) // 2, pair, (head0, head0))
    scores(sb_ref, k_tile(n_lat - 1))
    carry = consume(sa_ref, vl_ref[0, n_lat - 2], carry)
    scores(sc_ref, kc_ref[0])
    carry = consume(sb_ref, vl_ref[0, n_lat - 1], carry)
    (_, acc0), (_, acc1) = consume(sc_ref, vc_ref[0, 0], carry)
    ot = jnp.concatenate([acc[:MLA_V_HEAD] / acc[MLA_V_HEAD:MLA_V_HEAD + 1] for acc in (acc0, acc1)], axis=0)
    o_ref[0] = ot.T.astype(o_ref.dtype)


def _attention(q, k_lat, k_ctx, vt_lat, vt_ctx, *, tq):
    b, l, _ = q.shape
    c = k_ctx.shape[1]
    n_lat, _, tk = vt_lat.shape[1:]
    pairs = MLA_HEADS // 2
    return pl.pallas_call(
        _attn_kernel,
        grid=(b, pairs, l // tq),
        in_specs=[
            pl.BlockSpec((1, tq, 2 * HEAD_PAD), lambda bi, p, i: (bi, i, p)),
            pl.BlockSpec((1, l, 2 * HEAD_PAD), lambda bi, p, i: (bi, 0, p)),
            pl.BlockSpec((1, c, 2 * HEAD_PAD), lambda bi, p, i: (bi, 0, p)),
            pl.BlockSpec((1, n_lat, 2 * MLA_V_HEAD, tk), lambda bi, p, i: (bi, 0, p, 0)),
            pl.BlockSpec((1, 1, 2 * MLA_V_HEAD, c), lambda bi, p, i: (bi, 0, p, 0)),
        ],
        out_specs=pl.BlockSpec((1, tq, 2 * MLA_V_HEAD), lambda bi, p, i: (bi, i, p)),
        out_shape=jax.ShapeDtypeStruct((b, l, MLA_WIDTH), BF16),
        scratch_shapes=[pltpu.VMEM((2, tk, tq), F32), pltpu.VMEM((2, tk, tq), F32), pltpu.VMEM((2, c, tq), F32)],
        compiler_params=_params(3),
        name="attn",
    )(q, k_lat, k_ctx, vt_lat, vt_ctx)


def _gla_kernel(q_ref, k_ref, v_ref, g_ref, s0_ref, o_ref, sout_ref, st_scr, b_scr, *, reverse):
    i = pl.program_id(1)
    t = q_ref.shape[1]
    n_chunks = t // GLA_CHUNK
    grp = min(t, 256)

    @pl.when(i == 0)
    def _():
        st_scr[...] = s0_ref[0]

    r = lax.broadcasted_iota(jnp.int32, (grp, grp), 0)
    c = lax.broadcasted_iota(jnp.int32, (grp, grp), 1)
    same = (r // GLA_CHUNK) == (c // GLA_CHUNK)
    tri = jnp.where(same & ((c >= r) if reverse else (c <= r)), 1.0, 0.0).astype(BF16)
    for gi in range(t // grp):
        g = g_ref[0, gi * grp:(gi + 1) * grp, :]
        hi = g.astype(BF16)
        r1 = g - hi.astype(F32)
        mid = r1.astype(BF16)
        lo = (r1 - mid.astype(F32)).astype(BF16)
        b_scr[gi * grp:(gi + 1) * grp, :] = _dot(tri, hi) + _dot(tri, mid) + _dot(tri, lo)

    rr = lax.broadcasted_iota(jnp.int32, (GLA_CHUNK, GLA_CHUNK), 0)
    cc = lax.broadcasted_iota(jnp.int32, (GLA_CHUNK, GLA_CHUNK), 1)
    keep = (cc >= rr) if reverse else (cc <= rr)
    edge = 0 if reverse else GLA_CHUNK - 1

    def chunk(j, _):
        n = (n_chunks - 1 - j) if reverse else j
        rows = pl.ds(pl.multiple_of(n * GLA_CHUNK, GLA_CHUNK), GLA_CHUNK)
        for hh in range(GLA_HEADS):
            cols = slice(hh * GLA_HEAD_K, (hh + 1) * GLA_HEAD_K)
            bc = b_scr[rows, cols]
            b_all = bc[edge:edge + 1, :]
            qc = q_ref[0, rows, cols].astype(F32)
            kc = k_ref[0, rows, cols].astype(F32)
            vc = v_ref[0, rows, cols]
            qe = (qc * jnp.exp(bc)).astype(BF16)
            ke = (kc * jnp.exp(-bc)).astype(BF16)
            kd = (kc * jnp.exp(b_all - bc)).astype(BF16)
            st = st_scr[hh]
            a = jnp.where(keep, _dot_nt(qe, ke), 0.0).astype(BF16)
            o = _dot(a, vc) + _dot_nt(qe, st.astype(BF16))
            o_ref[0, rows, cols] = o.astype(o_ref.dtype)
            st_scr[hh] = st * jnp.exp(b_all) + _dot_tn(vc, kd)
        return 0

    lax.fori_loop(0, n_chunks, chunk, 0)

    @pl.when(i == pl.num_programs(1) - 1)
    def _():
        sout_ref[0] = st_scr[...]


def _gla(q, k, v, g, s0, *, reverse, tile):
    b, l, _ = q.shape
    tile = min(tile, l)
    nt = l // tile
    ti = (lambda i: nt - 1 - i) if reverse else (lambda i: i)
    gcol = 1 if reverse else 0
    tok = pl.BlockSpec((1, tile, GLA_KEY), lambda bi, i: (bi, ti(i), 0))
    st_spec = pl.BlockSpec((1, GLA_HEADS, GLA_HEAD_V, GLA_HEAD_K), lambda bi, i: (bi, 0, 0, 0))
    return pl.pallas_call(
        functools.partial(_gla_kernel, reverse=reverse),
        grid=(b, nt),
        in_specs=[tok, tok, tok,
                  pl.BlockSpec((1, tile, GLA_KEY), lambda bi, i: (bi, ti(i), gcol)),
                  st_spec],
        out_specs=[tok, st_spec],
        out_shape=[jax.ShapeDtypeStruct((b, l, GLA_VALUE), BF16),
                   jax.ShapeDtypeStruct((b, GLA_HEADS, GLA_HEAD_V, GLA_HEAD_K), F32)],
        scratch_shapes=[pltpu.VMEM((GLA_HEADS, GLA_HEAD_V, GLA_HEAD_K), F32),
                        pltpu.VMEM((tile, GLA_KEY), F32)],
        compiler_params=_params(2),
        name="gla_bwd" if reverse else "gla_fwd",
    )(q, k, v, g, s0)


def _mix_kernel(x_ref, sh_ref, sc_ref, gt_ref, g1_ref, om_ref, of_ref, ob_ref, wb_ref, gng_ref,
                wbm_ref, wbg_ref, wo_ref, o_ref):
    x = x_ref[0]
    h = (_rms(x, g1_ref[...]) * (1.0 + sc_ref[0]) + sh_ref[0]).astype(BF16)
    zr = _dot(h, wb_ref[:, 0:GLA_VALUE])
    og = of_ref[0].astype(F32) + ob_ref[0].astype(F32)
    ys = []
    for hh in range(GLA_HEADS):
        cols = slice(hh * GLA_HEAD_V, (hh + 1) * GLA_HEAD_V)
        r = zr[:, cols]
        ys.append((_rms(og[:, cols], gng_ref[...]) * (r * jax.nn.sigmoid(r))).astype(BF16))
    br_gla = _dot(jnp.concatenate(ys, axis=1), wbg_ref[...])
    br_mla = _dot(om_ref[0], wbm_ref[...])
    g_mla = jax.nn.sigmoid(_dot(h, wb_ref[:, GLA_VALUE:GLA_VALUE + D_MODEL]))
    g_gla = jax.nn.sigmoid(_dot(h, wb_ref[:, GLA_VALUE + D_MODEL:]))
    merged = (g_mla * br_mla + g_gla * br_gla).astype(BF16)
    o_ref[0] = x + gt_ref[0] * _dot(merged, wo_ref[...])


def _mix(x, mod3, g1, o_mla, o_f, o_b, w_b, gng, wbm, wbg, wo, *, tm):
    b, l, _ = x.shape
    tok = lambda w: pl.BlockSpec((1, tm, w), lambda bi, i: (bi, i, 0))
    modc = lambda j: pl.BlockSpec((1, 1, D_MODEL), lambda bi, i: (bi, 0, j))
    return pl.pallas_call(
        _mix_kernel,
        grid=(b, l // tm),
        in_specs=[tok(D_MODEL), modc(0), modc(1), modc(2), _const_spec((1, D_MODEL)),
                  tok(MLA_WIDTH), tok(GLA_VALUE), tok(GLA_VALUE),
                  _const_spec(w_b.shape), _const_spec(gng.shape), _const_spec(wbm.shape),
                  _const_spec(wbg.shape), _const_spec(wo.shape)],
        out_specs=tok(D_MODEL),
        out_shape=jax.ShapeDtypeStruct((b, l, D_MODEL), F32),
        compiler_params=_params(2),
        name="mix",
    )(x, mod3, mod3, mod3, g1, o_mla, o_f, o_b, w_b, gng, wbm, wbg, wo)


FF_CHUNK = 256


def _ffn_kernel(xm_ref, xp_ref, xn_ref, sh_ref, sc_ref, gt_ref, g2_ref, wv_ref, wg_ref, cw_ref, cb_ref,
                wd_ref, fg_ref, o_ref, he_scr, gate_scr, val_scr, act_scr, acc_scr):
    i = pl.program_id(1)
    tm = xm_ref.shape[1]
    te = tm + 2 * GRID_W
    n_chunks = D_FF // FF_CHUNK

    def normed(x, keep=None):
        h = _rms(x, g2_ref[...]) * (1.0 + sc_ref[0]) + sh_ref[0]
        return (h if keep is None else h * keep).astype(BF16)

    keep_prev = jnp.where(i > 0, 1.0, 0.0).astype(F32)
    keep_next = jnp.where(i < pl.num_programs(1) - 1, 1.0, 0.0).astype(F32)
    he_scr[0:GRID_W] = normed(xp_ref[0], keep_prev)
    he_scr[GRID_W:GRID_W + tm] = normed(xm_ref[0])
    he_scr[GRID_W + tm:te] = normed(xn_ref[0], keep_next)

    col = lax.broadcasted_iota(jnp.int32, (GRID_W, 1), 0)
    has_left = col > 0
    has_right = col < GRID_W - 1

    def project(c):
        cs = slice(c * FF_CHUNK, (c + 1) * FF_CHUNK)
        gate_scr[c % 2] = _dot(he_scr[...], wg_ref[:, cs])
        val_scr[c % 2] = _dot(he_scr[GRID_W:GRID_W + tm], wv_ref[:, cs])

    def activate(c):
        slot = c % 2
        for j in range(tm // GRID_W):
            for lh in range(FF_CHUNK // 128):
                lanes = slice(lh * 128, (lh + 1) * 128)
                wl = slice(c * FF_CHUNK + lh * 128, c * FF_CHUNK + (lh + 1) * 128)
                sums = [None, None, None]
                for dr in range(3):
                    blk = gate_scr[slot, (j + dr) * GRID_W:(j + dr + 1) * GRID_W, lanes]
                    for dc in range(3):
                        term = blk * cw_ref[3 * dr + dc:3 * dr + dc + 1, wl]
                        sums[dc] = term if sums[dc] is None else sums[dc] + term
                conv = (cb_ref[:, wl] + sums[1]
                        + jnp.where(has_left, pltpu.roll(sums[0], 1, 0), 0.0)
                        + jnp.where(has_right, pltpu.roll(sums[2], GRID_W - 1, 0), 0.0))
                gelu = 0.5 * conv * (1.0 + lax.erf(conv * np.float32(2.0 ** -0.5)))
                rows = slice(j * GRID_W, (j + 1) * GRID_W)
                act_scr[slot, rows, lanes] = (gelu * val_scr[slot, rows, lanes]).astype(BF16)

    def down(c):
        part = _dot(act_scr[c % 2], wd_ref[c * FF_CHUNK:(c + 1) * FF_CHUNK, :])
        if c == 0:
            acc_scr[...] = part
        else:
            acc_scr[...] += part

    project(0)
    for s in range(n_chunks):
        if s + 1 < n_chunks:
            project(s + 1)
        activate(s)
        down(s)
    x2 = xm_ref[0] + gt_ref[0] * acc_scr[...]
    o_ref[0] = _rms(x2, fg_ref[...])


def _ffn(x1, mod3, g2, wv, wg, cw, cb, wd, fg, *, tm):
    b, l, _ = x1.shape
    rpt = tm // GRID_W
    n_rows = l // GRID_W
    tok = lambda w: pl.BlockSpec((1, tm, w), lambda bi, i: (bi, i, 0))
    modc = lambda j: pl.BlockSpec((1, 1, D_MODEL), lambda bi, i: (bi, 0, j))
    return pl.pallas_call(
        _ffn_kernel,
        grid=(b, l // tm),
        in_specs=[tok(D_MODEL),
                  pl.BlockSpec((1, GRID_W, D_MODEL), lambda bi, i: (bi, jnp.maximum(i * rpt - 1, 0), 0)),
                  pl.BlockSpec((1, GRID_W, D_MODEL),
                               lambda bi, i: (bi, jnp.minimum((i + 1) * rpt, n_rows - 1), 0)),
                  modc(3), modc(4), modc(5), _const_spec((1, D_MODEL)),
                  _const_spec(wv.shape), _const_spec(wg.shape), _const_spec(cw.shape), _const_spec(cb.shape),
                  _const_spec(wd.shape), _const_spec((1, D_MODEL))],
        out_specs=tok(D_MODEL),
        out_shape=jax.ShapeDtypeStruct((b, l, D_MODEL), F32),
        scratch_shapes=[pltpu.VMEM((tm + 2 * GRID_W, D_MODEL), BF16),
                        pltpu.VMEM((2, tm + 2 * GRID_W, FF_CHUNK), F32),
                        pltpu.VMEM((2, tm, FF_CHUNK), F32),
                        pltpu.VMEM((2, tm, FF_CHUNK), BF16),
                        pltpu.VMEM((tm, D_MODEL), F32)],
        compiler_params=_params(2),
        name="ffn",
    )(x1, x1, x1, mod3, mod3, mod3, g2, wv, wg, cw, cb, wd, fg)


def _rope_tables(length):
    t = np.arange(length)
    inv_freq = ROPE_THETA ** (-np.arange(0, ROPE_AXIS_DIM, 2, dtype=np.float32) / ROPE_AXIS_DIM)
    inv_freq = jnp.asarray(inv_freq, F32)
    row = jnp.asarray(t // GRID_W, F32)[:, None] * inv_freq
    col = jnp.asarray(t % GRID_W, F32)[:, None] * inv_freq
    ang = jnp.concatenate([row, row, col, col], axis=1)
    cos, sin = jnp.cos(ang), jnp.sin(ang)
    z8 = jnp.zeros((length, ROPE_AXIS_DIM // 2), F32)
    pad = jnp.zeros((length, HEAD_PAD - MLA_ROPE), F32)
    tq = (MLA_SCALE * LOG2E) * jnp.concatenate([jnp.ones((length, MLA_NOPE), F32), cos, sin], axis=1)
    s_row, s_col = jnp.sin(row), jnp.sin(col)
    s_next = jnp.concatenate([-s_row, z8, -s_col, z8, pad], axis=1)
    s_prev = jnp.concatenate([z8, s_row, z8, s_col, pad], axis=1)
    return jnp.concatenate([tq, jnp.concatenate([cos, pad], axis=1), s_next, s_prev], axis=1)


def _ctx_tables(length):
    one = jnp.concatenate([jnp.ones((length, MLA_ROPE), F32), jnp.zeros((length, HEAD_PAD - MLA_ROPE), F32)], 1)
    z = jnp.zeros((length, HEAD_PAD), F32)
    return jnp.concatenate([z, one, z, z], axis=1)


def _rot_partner(w):
    h = ROPE_AXIS_DIM // 2
    r1, r2, c1, c2 = w[..., 0:h], w[..., h:2 * h], w[..., 2 * h:3 * h], w[..., 3 * h:4 * h]
    return jnp.concatenate([-r2, r1, -c2, c1], axis=-1)


def _layout_weights(w_in, w_uq, w_ukv, w_decay, b_decay):
    offs = np.cumsum((0, MLA_Q_LORA, MLA_KV_LORA, MLA_ROPE, GLA_KEY, GLA_KEY, GLA_VALUE, GLA_VALUE,
                      2 * GATE_RANK, 2 * D_MODEL))
    part = lambda j: w_in[:, offs[j]:offs[j + 1]]
    zc = lambda n: jnp.zeros((D_MODEL, n), F32)
    w_a = jnp.concatenate([part(0), part(1), part(2), part(7), zc(HEAD_PAD - MLA_ROPE - 2 * GATE_RANK),
                           part(3), part(4), part(5)], axis=1).astype(BF16)
    w_b = jnp.concatenate([part(6), part(8)], axis=1).astype(BF16)

    uq = w_uq.reshape(MLA_Q_LORA, MLA_HEADS, MLA_NOPE + MLA_ROPE)
    rope = uq[..., MLA_NOPE:]
    wuq = jnp.concatenate([uq, _rot_partner(rope)], axis=-1).reshape(MLA_Q_LORA, MLA_HEADS * HEAD_PAD)

    ukv = w_ukv.reshape(MLA_KV_LORA, MLA_HEADS, MLA_NOPE + MLA_V_HEAD)
    k_cols = jnp.concatenate([ukv[..., :MLA_NOPE], jnp.zeros((MLA_KV_LORA, MLA_HEADS, HEAD_PAD - MLA_NOPE), F32)],
                             axis=-1).reshape(MLA_KV_LORA, MLA_HEADS * HEAD_PAD)
    v_cols = ukv[..., MLA_NOPE:].reshape(MLA_KV_LORA, MLA_WIDTH)
    eye = jnp.eye(HEAD_PAD, MLA_ROPE, dtype=F32)
    route = jnp.concatenate([jnp.zeros((HEAD_PAD, MLA_NOPE), F32), eye, eye], axis=1)
    route = jnp.tile(route, (1, MLA_HEADS))
    wk = jnp.concatenate([k_cols, route], axis=0)

    wd = jnp.zeros((HEAD_PAD, 2 * GLA_KEY), F32)
    wd = wd.at[MLA_ROPE:MLA_ROPE + GATE_RANK, :GLA_KEY].set(w_decay[0])
    wd = wd.at[MLA_ROPE + GATE_RANK:MLA_ROPE + 2 * GATE_RANK, GLA_KEY:].set(w_decay[1])
    bd = b_decay.reshape(1, 2 * GLA_KEY)
    return w_a, w_b, wuq.astype(BF16), wk.astype(BF16), v_cols.T.astype(BF16), wd.astype(BF16), bd


def kernel(x, c, ctx, c_ctx, w_ada, b_ada, norm1_g, w_in, q_norm_g, w_uq, kv_norm_g, w_ukv, gla_w_decay,
           gla_b_decay, gla_norm_g, w_br_mla, w_br_gla, w_out, norm2_g, w_up, conv_w, conv_b, w_down, final_g):
    b, l, _ = x.shape
    n_ctx = ctx.shape[1]
    assert w_ada.shape[0] == 1, "single layer: context tokens are never updated"
    row2 = lambda a: a.reshape(1, -1)

    c16 = jnp.concatenate([c, c_ctx[None], jnp.zeros((16 - b - 1, D_MODEL), F32)], axis=0)
    mod3 = _ada(c16, w_ada[0], b_ada[0]).reshape(16, 1, 6 * D_MODEL)

    w_a, w_b, wuq, wk, wvt, wd, bd = _layout_weights(w_in[0], w_uq[0], w_ukv[0], gla_w_decay[0], gla_b_decay[0])
    g1 = row2(norm1_g[0])
    proj = functools.partial(_inproj, g1=g1, w_a=w_a, qg=row2(q_norm_g[0]), wuq=wuq, kvg=row2(kv_norm_g[0]),
                             wk=wk, wvt=wvt, wd=wd, bd=bd, tm=512)
    q, k, v, gq, gk, gv, g = proj(x, mod3, None, tab=_rope_tables(l), with_q=True)
    k_c, v_c, gk_c, gv_c, g_c = proj(ctx, mod3, b, tab=_ctx_tables(n_ctx), with_q=False)

    o_mla = _attention(q, k, k_c, v, v_c, tq=1024)

    s0 = jnp.zeros((b, GLA_HEADS, GLA_HEAD_V, GLA_HEAD_K), F32)
    zq = jnp.zeros((b, n_ctx, GLA_KEY), BF16)
    _, s_f = _gla(zq, gk_c, gv_c, g_c, s0, reverse=False, tile=256)
    _, s_b = _gla(zq, gk_c, gv_c, g_c, s0, reverse=True, tile=256)
    o_f, _ = _gla(gq, gk, gv, g, s_f, reverse=False, tile=512)
    o_b, _ = _gla(gq, gk, gv, g, s_b, reverse=True, tile=512)

    x1 = _mix(x, mod3, g1, o_mla, o_f, o_b, w_b, row2(gla_norm_g[0]), w_br_mla[0].astype(BF16),
              w_br_gla[0].astype(BF16), w_out[0].astype(BF16), tm=512)

    w_up_b = w_up[0].astype(BF16)
    return _ffn(x1, mod3, row2(norm2_g[0]), w_up_b[:, :D_FF], w_up_b[:, D_FF:], conv_w[0].reshape(9, D_FF),
                row2(conv_b[0]), w_down[0].astype(BF16), row2(final_g), tm=512)
```

```python
import functools

import numpy as np
import jax
import jax.numpy as jnp
from jax import lax
from jax.experimental import pallas as pl
from jax.experimental.pallas import tpu as pltpu

F32 = jnp.float32
BF16 = jnp.bfloat16

D_MODEL = 1024
GRID_W = 64
EPS = 1e-6
MLA_HEADS = 8
MLA_NOPE = 64
MLA_ROPE = 32
MLA_V_HEAD = 64
MLA_Q_LORA = 384
MLA_KV_LORA = 256
MLA_WIDTH = MLA_HEADS * MLA_V_HEAD
MLA_SCALE = (MLA_NOPE + MLA_ROPE) ** -0.5
ROPE_AXIS_DIM = MLA_ROPE // 2
ROPE_THETA = 10000.0
GLA_HEADS = 4
GLA_HEAD_K = 128
GLA_HEAD_V = 128
GLA_KEY = GLA_HEADS * GLA_HEAD_K
GLA_VALUE = GLA_HEADS * GLA_HEAD_V
GLA_QSCALE = GLA_HEAD_K ** -0.5
GATE_RANK = 16
GATE_NORM = 16.0
GLA_CHUNK = 64
D_FF = 2816
LOG2E = 1.4426950408889634
ONES_ROWS = 16
HEAD_PAD = 128

VMEM_LIMIT = 56 * 1024 * 1024

A_Q = (0, 384)
A_KV = (384, 640)
A_MISC = (640, 768)
A_GQ = (768, 1280)
A_GK = (1280, 1792)
A_GV = (1792, 2304)
A_COLS = 2304


def _dot(a, b):
    return jnp.dot(a, b, preferred_element_type=F32)


def _dot_nt(a, b):
    return lax.dot_general(a, b, (((1,), (1,)), ((), ())), preferred_element_type=F32)


def _dot_tn(a, b):
    return lax.dot_general(a, b, (((0,), (0,)), ((), ())), preferred_element_type=F32)


def _rms(x, g):
    return x * lax.rsqrt(jnp.mean(x * x, axis=-1, keepdims=True) + EPS) * g


def _const_spec(shape):
    nd = len(shape)
    return pl.BlockSpec(shape, lambda *_: (0,) * nd)


def _params(n_grid):
    return pltpu.CompilerParams(dimension_semantics=("arbitrary",) * n_grid, vmem_limit_bytes=VMEM_LIMIT)


def _ada_kernel(c_ref, w_ref, b_ref, o_ref):
    c = c_ref[...]
    s = (c * jax.nn.sigmoid(c)).astype(BF16)
    o_ref[...] = _dot(s, w_ref[...].astype(BF16)) + b_ref[...]


def _ada(c16, w_ada, b_ada):
    n = w_ada.shape[1]
    bn = 512
    return pl.pallas_call(
        _ada_kernel,
        grid=(n // bn,),
        in_specs=[_const_spec((16, D_MODEL)),
                  pl.BlockSpec((D_MODEL, bn), lambda j: (0, j)),
                  pl.BlockSpec((1, bn), lambda j: (0, j))],
        out_specs=pl.BlockSpec((16, bn), lambda j: (0, j)),
        out_shape=jax.ShapeDtypeStruct((16, n), F32),
        compiler_params=_params(1),
        name="ada",
    )(c16, w_ada, b_ada.reshape(1, n))


def _inproj_kernel(x_ref, sh_ref, sc_ref, g1_ref, tab_ref, w_ref, qg_ref, wuq_ref, kvg_ref, wk_ref, wvt_ref,
                   wd_ref, bd_ref, *out_refs, with_q):
    if with_q:
        q_ref, k_ref, v_ref, gq_ref, gk_ref, gv_ref, g_ref = out_refs
    else:
        k_ref, v_ref, gk_ref, gv_ref, g_ref = out_refs
    x = x_ref[0]
    h = (_rms(x, g1_ref[...]) * (1.0 + sc_ref[0]) + sh_ref[0]).astype(BF16)

    if with_q:
        zq = _dot(h, w_ref[:, A_Q[0]:A_Q[1]])
        qn = _rms(zq, qg_ref[...]).astype(BF16)
        qf = _dot(qn, wuq_ref[...])
        tq = tab_ref[:, 0:HEAD_PAD]
        for hh in range(MLA_HEADS):
            sl = slice(hh * HEAD_PAD, (hh + 1) * HEAD_PAD)
            q_ref[0, :, sl] = (qf[:, sl] * tq).astype(BF16)
        gq_ref[0] = (_dot(h, w_ref[:, A_GQ[0]:A_GQ[1]]) * GLA_QSCALE).astype(BF16)

    zkv = _dot(h, w_ref[:, A_KV[0]:A_KV[1]])
    kvn = _rms(zkv, kvg_ref[...]).astype(BF16)
    misc = _dot(h, w_ref[:, A_MISC[0]:A_MISC[1]])
    kr = (misc * tab_ref[:, 128:256]
          + pltpu.roll(misc, HEAD_PAD - 8, 1) * tab_ref[:, 256:384]
          + pltpu.roll(misc, 8, 1) * tab_ref[:, 384:512])
    k_ref[0] = _dot(jnp.concatenate([kvn, kr.astype(BF16)], axis=1), wk_ref[...]).astype(BF16)
    v_ref[0, 0] = _dot_nt(wvt_ref[...], kvn).astype(BF16)

    gk_ref[0] = _dot(h, w_ref[:, A_GK[0]:A_GK[1]]).astype(BF16)
    gv_ref[0] = _dot(h, w_ref[:, A_GV[0]:A_GV[1]]).astype(BF16)

    gp = _dot(misc.astype(BF16), wd_ref[...]) + bd_ref[...]
    g_ref[0] = (jnp.minimum(gp, 0.0) - jnp.log1p(jnp.exp(-jnp.abs(gp)))) * (1.0 / GATE_NORM)


def _inproj(x, mod3, mod_row, g1, tab, w_a, qg, wuq, kvg, wk, wvt, wd, bd, *, tm, with_q):
    b, l, _ = x.shape
    tm = min(tm, l)
    nt = l // tm
    row = (lambda bi: bi) if mod_row is None else (lambda bi: mod_row)
    tok = lambda w: pl.BlockSpec((1, tm, w), lambda bi, i: (bi, i, 0))
    in_specs = [
        tok(D_MODEL),
        pl.BlockSpec((1, 1, D_MODEL), lambda bi, i: (row(bi), 0, 0)),
        pl.BlockSpec((1, 1, D_MODEL), lambda bi, i: (row(bi), 0, 1)),
        _const_spec((1, D_MODEL)),
        pl.BlockSpec((tm, 512), lambda bi, i: (i, 0)),
        _const_spec(w_a.shape), _const_spec(qg.shape), _const_spec(wuq.shape), _const_spec(kvg.shape),
        _const_spec(wk.shape), _const_spec(wvt.shape), _const_spec(wd.shape), _const_spec(bd.shape),
    ]
    sds = lambda w, dt: jax.ShapeDtypeStruct((b, l, w), dt)
    kw = MLA_HEADS * HEAD_PAD
    outs = [(kw, BF16), None, (GLA_KEY, BF16), (GLA_VALUE, BF16), (2 * GLA_KEY, F32)]
    if with_q:
        outs = [(kw, BF16)] + outs[:2] + [(GLA_KEY, BF16)] + outs[2:]
    vt_spec = pl.BlockSpec((1, 1, MLA_WIDTH, tm), lambda bi, i: (bi, i, 0, 0))
    vt_shape = jax.ShapeDtypeStruct((b, nt, MLA_WIDTH, tm), BF16)
    return pl.pallas_call(
        functools.partial(_inproj_kernel, with_q=with_q),
        grid=(b, nt),
        in_specs=in_specs,
        out_specs=[vt_spec if o is None else tok(o[0]) for o in outs],
        out_shape=[vt_shape if o is None else sds(*o) for o in outs],
        compiler_params=_params(2),
        name="inproj_lat" if with_q else "inproj_ctx",
    )(x, mod3, mod3, g1, tab, w_a, qg, wuq, kvg, wk, wvt, wd, bd)


def _attn_kernel(q_ref, kl_ref, kc_ref, vl_ref, vc_ref, o_ref, sa_ref, sb_ref, sc_ref):
    tq = q_ref.shape[1]
    n_lat, _, tk = vl_ref.shape[1:]
    assert n_lat >= 2 and n_lat % 2 == 0

    def scores(dst, k_t):
        for hh in range(2):
            cols = slice(hh * HEAD_PAD, (hh + 1) * HEAD_PAD)
            dst[hh] = _dot_nt(k_t[:, cols], q_ref[0, :, cols])

    def consume(src, vt, carry):
        ones = jnp.ones((ONES_ROWS, vt.shape[1]), BF16)
        out = []
        for hh in range(2):
            m, acc = carry[hh]
            s = src[hh]
            m_new = jnp.maximum(m, jnp.max(s, axis=0, keepdims=True))
            p = jnp.exp2(s - m_new).astype(BF16)
            v_aug = jnp.concatenate([vt[hh * MLA_V_HEAD:(hh + 1) * MLA_V_HEAD], ones], axis=0)
            out.append((m_new, acc * jnp.exp2(m - m_new) + _dot(v_aug, p)))
        return tuple(out)

    def k_tile(i):
        return kl_ref[0, pl.ds(pl.multiple_of(i * tk, tk), tk), :]

    def pair(it, carry):
        j = 2 * it
        scores(sb_ref, k_tile(j + 1))
        carry = consume(sa_ref, vl_ref[0, j], carry)
        scores(sa_ref, k_tile(j + 2))
        return consume(sb_ref, vl_ref[0, j + 1], carry)

    head0 = (jnp.full((1, tq), -jnp.inf, F32), jnp.zeros((MLA_V_HEAD + ONES_ROWS, tq), F32))
    scores(sa_ref, k_tile(0))
    carry = lax.fori_loop(0, (n_lat - 2) // 2, pair, (head0, head0))
    scores(sb_ref, k_tile(n_lat - 1))
    carry = consume(sa_ref, vl_ref[0, n_lat - 2], carry)
    scores(sc_ref, kc_ref[0])
    carry = consume(sb_ref, vl_ref[0, n_lat - 1], carry)
    (_, acc0), (_, acc1) = consume(sc_ref, vc_ref[0, 0], carry)
    ot = jnp.concatenate([acc[:MLA_V_HEAD] / acc[MLA_V_HEAD:MLA_V_HEAD + 1] for acc in (acc0, acc1)], axis=0)
    o_ref[0] = ot.T.astype(o_ref.dtype)


def _attention(q, k_lat, k_ctx, vt_lat, vt_ctx, *, tq):
    b, l, _ = q.shape
    c = k_ctx.shape[1]
    n_lat, _, tk = vt_lat.shape[1:]
    pairs = MLA_HEADS // 2
    return pl.pallas_call(
        _attn_kernel,
        grid=(b, pairs, l // tq),
        in_specs=[
            pl.BlockSpec((1, tq, 2 * HEAD_PAD), lambda bi, p, i: (bi, i, p)),
            pl.BlockSpec((1, l, 2 * HEAD_PAD), lambda bi, p, i: (bi, 0, p)),
            pl.BlockSpec((1, c, 2 * HEAD_PAD), lambda bi, p, i: (bi, 0, p)),
            pl.BlockSpec((1, n_lat, 2 * MLA_V_HEAD, tk), lambda bi, p, i: (bi, 0, p, 0)),
            pl.BlockSpec((1, 1, 2 * MLA_V_HEAD, c), lambda bi, p, i: (bi, 0, p, 0)),
        ],
        out_specs=pl.BlockSpec((1, tq, 2 * MLA_V_HEAD), lambda bi, p, i: (bi, i, p)),
        out_shape=jax.ShapeDtypeStruct((b, l, MLA_WIDTH), BF16),
        scratch_shapes=[pltpu.VMEM((2, tk, tq), F32), pltpu.VMEM((2, tk, tq), F32), pltpu.VMEM((2, c, tq), F32)],
        compiler_params=_params(3),
        name="attn",
    )(q, k_lat, k_ctx, vt_lat, vt_ctx)


def _gla_kernel(qf_ref, kf_ref, vf_ref, gf_ref, qb_ref, kb_ref, vb_ref, gb_ref, s0_ref,
                of_ref, ob_ref, sout_ref, st_scr, b_scr):
    i = pl.program_id(1)
    t = qf_ref.shape[1]
    n_chunks = t // GLA_CHUNK
    grp = min(t, 256)
    dirs = ((qf_ref, kf_ref, vf_ref, gf_ref, of_ref), (qb_ref, kb_ref, vb_ref, gb_ref, ob_ref))

    @pl.when(i == 0)
    def _():
        st_scr[...] = s0_ref[:, 0]

    r = lax.broadcasted_iota(jnp.int32, (grp, grp), 0)
    c = lax.broadcasted_iota(jnp.int32, (grp, grp), 1)
    same = (r // GLA_CHUNK) == (c // GLA_CHUNK)
    for d, refs in enumerate(dirs):
        tri = jnp.where(same & ((c >= r) if d else (c <= r)), 1.0, 0.0).astype(BF16)
        for gi in range(t // grp):
            g = refs[3][0, gi * grp:(gi + 1) * grp, :]
            hi = g.astype(BF16)
            r1 = g - hi.astype(F32)
            mid = r1.astype(BF16)
            lo = (r1 - mid.astype(F32)).astype(BF16)
            b_scr[d, gi * grp:(gi + 1) * grp, :] = _dot(tri, hi) + _dot(tri, mid) + _dot(tri, lo)

    rr = lax.broadcasted_iota(jnp.int32, (GLA_CHUNK, GLA_CHUNK), 0)
    cc = lax.broadcasted_iota(jnp.int32, (GLA_CHUNK, GLA_CHUNK), 1)

    def chunk(j, _):
        for d, (q_ref, k_ref, v_ref, _, o_ref) in enumerate(dirs):
            n = (n_chunks - 1 - j) if d else j
            keep = (cc >= rr) if d else (cc <= rr)
            edge = 0 if d else GLA_CHUNK - 1
            rows = pl.ds(pl.multiple_of(n * GLA_CHUNK, GLA_CHUNK), GLA_CHUNK)
            for hh in range(GLA_HEADS):
                cols = slice(hh * GLA_HEAD_K, (hh + 1) * GLA_HEAD_K)
                bc = b_scr[d, rows, cols]
                b_all = bc[edge:edge + 1, :]
                qc = q_ref[0, rows, cols].astype(F32)
                kc = k_ref[0, rows, cols].astype(F32)
                vc = v_ref[0, rows, cols]
                qe = (qc * jnp.exp(bc)).astype(BF16)
                ke = (kc * jnp.exp(-bc)).astype(BF16)
                kd = (kc * jnp.exp(b_all - bc)).astype(BF16)
                st = st_scr[d, hh]
                a = jnp.where(keep, _dot_nt(qe, ke), 0.0).astype(BF16)
                o = _dot(a, vc) + _dot_nt(qe, st.astype(BF16))
                o_ref[0, rows, cols] = o.astype(o_ref.dtype)
                st_scr[d, hh] = st * jnp.exp(b_all) + _dot_tn(vc, kd)
        return 0

    lax.fori_loop(0, n_chunks, chunk, 0, unroll=2)

    @pl.when(i == pl.num_programs(1) - 1)
    def _():
        sout_ref[:, 0] = st_scr[...]


def _gla(q, k, v, g, s0, *, tile):
    b, l, _ = q.shape
    tile = min(tile, l)
    nt = l // tile
    fwd = pl.BlockSpec((1, tile, GLA_KEY), lambda bi, i: (bi, i, 0))
    bwd = pl.BlockSpec((1, tile, GLA_KEY), lambda bi, i: (bi, nt - 1 - i, 0))
    st_spec = pl.BlockSpec((2, 1, GLA_HEADS, GLA_HEAD_V, GLA_HEAD_K), lambda bi, i: (0, bi, 0, 0, 0))
    return pl.pallas_call(
        _gla_kernel,
        grid=(b, nt),
        in_specs=[fwd, fwd, fwd, fwd, bwd, bwd, bwd,
                  pl.BlockSpec((1, tile, GLA_KEY), lambda bi, i: (bi, nt - 1 - i, 1)),
                  st_spec],
        out_specs=[fwd, bwd, st_spec],
        out_shape=[jax.ShapeDtypeStruct((b, l, GLA_VALUE), BF16), jax.ShapeDtypeStruct((b, l, GLA_VALUE), BF16),
                   jax.ShapeDtypeStruct((2, b, GLA_HEADS, GLA_HEAD_V, GLA_HEAD_K), F32)],
        scratch_shapes=[pltpu.VMEM((2, GLA_HEADS, GLA_HEAD_V, GLA_HEAD_K), F32),
                        pltpu.VMEM((2, tile, GLA_KEY), F32)],
        compiler_params=_params(2),
        name="gla",
    )(q, k, v, g, q, k, v, g, s0)


def _mix_kernel(x_ref, sh_ref, sc_ref, gt_ref, g1_ref, om_ref, of_ref, ob_ref, wb_ref, gng_ref,
                wbm_ref, wbg_ref, wo_ref, o_ref):
    x = x_ref[0]
    h = (_rms(x, g1_ref[...]) * (1.0 + sc_ref[0]) + sh_ref[0]).astype(BF16)
    zr = _dot(h, wb_ref[:, 0:GLA_VALUE])
    og = of_ref[0].astype(F32) + ob_ref[0].astype(F32)
    ys = []
    for hh in range(GLA_HEADS):
        cols = slice(hh * GLA_HEAD_V, (hh + 1) * GLA_HEAD_V)
        r = zr[:, cols]
        ys.append((_rms(og[:, cols], gng_ref[...]) * (r * jax.nn.sigmoid(r))).astype(BF16))
    br_gla = _dot(jnp.concatenate(ys, axis=1), wbg_ref[...])
    br_mla = _dot(om_ref[0], wbm_ref[...])
    g_mla = jax.nn.sigmoid(_dot(h, wb_ref[:, GLA_VALUE:GLA_VALUE + D_MODEL]))
    g_gla = jax.nn.sigmoid(_dot(h, wb_ref[:, GLA_VALUE + D_MODEL:]))
    merged = (g_mla * br_mla + g_gla * br_gla).astype(BF16)
    o_ref[0] = x + gt_ref[0] * _dot(merged, wo_ref[...])


def _mix(x, mod3, g1, o_mla, o_f, o_b, w_b, gng, wbm, wbg, wo, *, tm):
    b, l, _ = x.shape
    tok = lambda w: pl.BlockSpec((1, tm, w), lambda bi, i: (bi, i, 0))
    modc = lambda j: pl.BlockSpec((1, 1, D_MODEL), lambda bi, i: (bi, 0, j))
    return pl.pallas_call(
        _mix_kernel,
        grid=(b, l // tm),
        in_specs=[tok(D_MODEL), modc(0), modc(1), modc(2), _const_spec((1, D_MODEL)),
                  tok(MLA_WIDTH), tok(GLA_VALUE), tok(GLA_VALUE),
                  _const_spec(w_b.shape), _const_spec(gng.shape), _const_spec(wbm.shape),
                  _const_spec(wbg.shape), _const_spec(wo.shape)],
        out_specs=tok(D_MODEL),
        out_shape=jax.ShapeDtypeStruct((b, l, D_MODEL), F32),
        compiler_params=_params(2),
        name="mix",
    )(x, mod3, mod3, mod3, g1, o_mla, o_f, o_b, w_b, gng, wbm, wbg, wo)


FF_CHUNK = 256


def _ffn_kernel(xm_ref, xp_ref, xn_ref, sh_ref, sc_ref, gt_ref, g2_ref, wv_ref, wg_ref, cw_ref, cb_ref,
                wd_ref, fg_ref, o_ref, he_scr, gate_scr, val_scr, act_scr, acc_scr):
    i = pl.program_id(1)
    tm = xm_ref.shape[1]
    te = tm + 2 * GRID_W
    n_chunks = D_FF // FF_CHUNK

    def normed(x, keep=None):
        h = _rms(x, g2_ref[...]) * (1.0 + sc_ref[0]) + sh_ref[0]
        return (h if keep is None else h * keep).astype(BF16)

    keep_prev = jnp.where(i > 0, 1.0, 0.0).astype(F32)
    keep_next = jnp.where(i < pl.num_programs(1) - 1, 1.0, 0.0).astype(F32)
    he_scr[0:GRID_W] = normed(xp_ref[0], keep_prev)
    he_scr[GRID_W:GRID_W + tm] = normed(xm_ref[0])
    he_scr[GRID_W + tm:te] = normed(xn_ref[0], keep_next)

    col = lax.broadcasted_iota(jnp.int32, (GRID_W, 1), 0)
    has_left = col > 0
    has_right = col < GRID_W - 1

    def project(c):
        cs = slice(c * FF_CHUNK, (c + 1) * FF_CHUNK)
        gate_scr[c % 2] = _dot(he_scr[...], wg_ref[:, cs])
        val_scr[c % 2] = _dot(he_scr[GRID_W:GRID_W + tm], wv_ref[:, cs])

    def activate(c):
        slot = c % 2
        for j in range(tm // GRID_W):
            for lh in range(FF_CHUNK // 128):
                lanes = slice(lh * 128, (lh + 1) * 128)
                wl = slice(c * FF_CHUNK + lh * 128, c * FF_CHUNK + (lh + 1) * 128)
                sums = [None, None, None]
                for dr in range(3):
                    blk = gate_scr[slot, (j + dr) * GRID_W:(j + dr + 1) * GRID_W, lanes]
                    for dc in range(3):
                        term = blk * cw_ref[3 * dr + dc:3 * dr + dc + 1, wl]
                        sums[dc] = term if sums[dc] is None else sums[dc] + term
                conv = (cb_ref[:, wl] + sums[1]
                        + jnp.where(has_left, pltpu.roll(sums[0], 1, 0), 0.0)
                        + jnp.where(has_right, pltpu.roll(sums[2], GRID_W - 1, 0), 0.0))
                gelu = 0.5 * conv * (1.0 + lax.erf(conv * np.float32(2.0 ** -0.5)))
                rows = slice(j * GRID_W, (j + 1) * GRID_W)
                act_scr[slot, rows, lanes] = (gelu * val_scr[slot, rows, lanes]).astype(BF16)

    def down(c):
        part = _dot(act_scr[c % 2], wd_ref[c * FF_CHUNK:(c + 1) * FF_CHUNK, :])
        if c == 0:
            acc_scr[...] = part
        else:
            acc_scr[...] += part

    project(0)
    for s in range(n_chunks):
        if s + 1 < n_chunks:
            project(s + 1)
        activate(s)
        down(s)
    x2 = xm_ref[0] + gt_ref[0] * acc_scr[...]
    o_ref[0] = _rms(x2, fg_ref[...])


def _ffn(x1, mod3, g2, wv, wg, cw, cb, wd, fg, *, tm):
    b, l, _ = x1.shape
    rpt = tm // GRID_W
    n_rows = l // GRID_W
    tok = lambda w: pl.BlockSpec((1, tm, w), lambda bi, i: (bi, i, 0))
    modc = lambda j: pl.BlockSpec((1, 1, D_MODEL), lambda bi, i: (bi, 0, j))
    return pl.pallas_call(
        _ffn_kernel,
        grid=(b, l // tm),
        in_specs=[tok(D_MODEL),
                  pl.BlockSpec((1, GRID_W, D_MODEL), lambda bi, i: (bi, jnp.maximum(i * rpt - 1, 0), 0)),
                  pl.BlockSpec((1, GRID_W, D_MODEL),
                               lambda bi, i: (bi, jnp.minimum((i + 1) * rpt, n_rows - 1), 0)),
                  modc(3), modc(4), modc(5), _const_spec((1, D_MODEL)),
                  _const_spec(wv.shape), _const_spec(wg.shape), _const_spec(cw.shape), _const_spec(cb.shape),
                  _const_spec(wd.shape), _const_spec((1, D_MODEL))],
        out_specs=tok(D_MODEL),
        out_shape=jax.ShapeDtypeStruct((b, l, D_MODEL), F32),
        scratch_shapes=[pltpu.VMEM((tm + 2 * GRID_W, D_MODEL), BF16),
                        pltpu.VMEM((2, tm + 2 * GRID_W, FF_CHUNK), F32),
                        pltpu.VMEM((2, tm, FF_CHUNK), F32),
                        pltpu.VMEM((2, tm, FF_CHUNK), BF16),
                        pltpu.VMEM((tm, D_MODEL), F32)],
        compiler_params=_params(2),
        name="ffn",
    )(x1, x1, x1, mod3, mod3, mod3, g2, wv, wg, cw, cb, wd, fg)


def _rope_tables(length):
    t = np.arange(length)
    inv_freq = ROPE_THETA ** (-np.arange(0, ROPE_AXIS_DIM, 2, dtype=np.float32) / ROPE_AXIS_DIM)
    inv_freq = jnp.asarray(inv_freq, F32)
    row = jnp.asarray(t // GRID_W, F32)[:, None] * inv_freq
    col = jnp.asarray(t % GRID_W, F32)[:, None] * inv_freq
    ang = jnp.concatenate([row, row, col, col], axis=1)
    cos, sin = jnp.cos(ang), jnp.sin(ang)
    z8 = jnp.zeros((length, ROPE_AXIS_DIM // 2), F32)
    pad = jnp.zeros((length, HEAD_PAD - MLA_ROPE), F32)
    tq = (MLA_SCALE * LOG2E) * jnp.concatenate([jnp.ones((length, MLA_NOPE), F32), cos, sin], axis=1)
    s_row, s_col = jnp.sin(row), jnp.sin(col)
    s_next = jnp.concatenate([-s_row, z8, -s_col, z8, pad], axis=1)
    s_prev = jnp.concatenate([z8, s_row, z8, s_col, pad], axis=1)
    return jnp.concatenate([tq, jnp.concatenate([cos, pad], axis=1), s_next, s_prev], axis=1)


def _ctx_tables(length):
    one = jnp.concatenate([jnp.ones((length, MLA_ROPE), F32), jnp.zeros((length, HEAD_PAD - MLA_ROPE), F32)], 1)
    z = jnp.zeros((length, HEAD_PAD), F32)
    return jnp.concatenate([z, one, z, z], axis=1)


def _rot_partner(w):
    h = ROPE_AXIS_DIM // 2
    r1, r2, c1, c2 = w[..., 0:h], w[..., h:2 * h], w[..., 2 * h:3 * h], w[..., 3 * h:4 * h]
    return jnp.concatenate([-r2, r1, -c2, c1], axis=-1)


def _layout_weights(w_in, w_uq, w_ukv, w_decay, b_decay):
    offs = np.cumsum((0, MLA_Q_LORA, MLA_KV_LORA, MLA_ROPE, GLA_KEY, GLA_KEY, GLA_VALUE, GLA_VALUE,
                      2 * GATE_RANK, 2 * D_MODEL))
    part = lambda j: w_in[:, offs[j]:offs[j + 1]]
    zc = lambda n: jnp.zeros((D_MODEL, n), F32)
    w_a = jnp.concatenate([part(0), part(1), part(2), part(7), zc(HEAD_PAD - MLA_ROPE - 2 * GATE_RANK),
                           part(3), part(4), part(5)], axis=1).astype(BF16)
    w_b = jnp.concatenate([part(6), part(8)], axis=1).astype(BF16)

    uq = w_uq.reshape(MLA_Q_LORA, MLA_HEADS, MLA_NOPE + MLA_ROPE)
    rope = uq[..., MLA_NOPE:]
    wuq = jnp.concatenate([uq, _rot_partner(rope)], axis=-1).reshape(MLA_Q_LORA, MLA_HEADS * HEAD_PAD)

    ukv = w_ukv.reshape(MLA_KV_LORA, MLA_HEADS, MLA_NOPE + MLA_V_HEAD)
    k_cols = jnp.concatenate([ukv[..., :MLA_NOPE], jnp.zeros((MLA_KV_LORA, MLA_HEADS, HEAD_PAD - MLA_NOPE), F32)],
                             axis=-1).reshape(MLA_KV_LORA, MLA_HEADS * HEAD_PAD)
    v_cols = ukv[..., MLA_NOPE:].reshape(MLA_KV_LORA, MLA_WIDTH)
    eye = jnp.eye(HEAD_PAD, MLA_ROPE, dtype=F32)
    route = jnp.concatenate([jnp.zeros((HEAD_PAD, MLA_NOPE), F32), eye, eye], axis=1)
    route = jnp.tile(route, (1, MLA_HEADS))
    wk = jnp.concatenate([k_cols, route], axis=0)

    wd = jnp.zeros((HEAD_PAD, 2 * GLA_KEY), F32)
    wd = wd.at[MLA_ROPE:MLA_ROPE + GATE_RANK, :GLA_KEY].set(w_decay[0])
    wd = wd.at[MLA_ROPE + GATE_RANK:MLA_ROPE + 2 * GATE_RANK, GLA_KEY:].set(w_decay[1])
    bd = b_decay.reshape(1, 2 * GLA_KEY)
    return w_a, w_b, wuq.astype(BF16), wk.astype(BF16), v_cols.T.astype(BF16), wd.astype(BF16), bd


def kernel(x, c, ctx, c_ctx, w_ada, b_ada, norm1_g, w_in, q_norm_g, w_uq, kv_norm_g, w_ukv, gla_w_decay,
           gla_b_decay, gla_norm_g, w_br_mla, w_br_gla, w_out, norm2_g, w_up, conv_w, conv_b, w_down, final_g):
    b, l, _ = x.shape
    n_ctx = ctx.shape[1]
    assert w_ada.shape[0] == 1, "single layer: context tokens are never updated"
    row2 = lambda a: a.reshape(1, -1)

    c16 = jnp.concatenate([c, c_ctx[None], jnp.zeros((16 - b - 1, D_MODEL), F32)], axis=0)
    mod3 = _ada(c16, w_ada[0], b_ada[0]).reshape(16, 1, 6 * D_MODEL)

    w_a, w_b, wuq, wk, wvt, wd, bd = _layout_weights(w_in[0], w_uq[0], w_ukv[0], gla_w_decay[0], gla_b_decay[0])
    g1 = row2(norm1_g[0])
    proj = functools.partial(_inproj, g1=g1, w_a=w_a, qg=row2(q_norm_g[0]), wuq=wuq, kvg=row2(kv_norm_g[0]),
                             wk=wk, wvt=wvt, wd=wd, bd=bd, tm=512)
    q, k, v, gq, gk, gv, g = proj(x, mod3, None, tab=_rope_tables(l), with_q=True)
    k_c, v_c, gk_c, gv_c, g_c = proj(ctx, mod3, b, tab=_ctx_tables(n_ctx), with_q=False)

    o_mla = _attention(q, k, k_c, v, v_c, tq=1024)

    s0 = jnp.zeros((2, b, GLA_HEADS, GLA_HEAD_V, GLA_HEAD_K), F32)
    zq = jnp.zeros((b, n_ctx, GLA_KEY), BF16)
    _, _, s_ctx = _gla(zq, gk_c, gv_c, g_c, s0, tile=256)
    o_f, o_b, _ = _gla(gq, gk, gv, g, s_ctx, tile=512)

    x1 = _mix(x, mod3, g1, o_mla, o_f, o_b, w_b, row2(gla_norm_g[0]), w_br_mla[0].astype(BF16),
              w_br_gla[0].astype(BF16), w_out[0].astype(BF16), tm=512)

    w_up_b = w_up[0].astype(BF16)
    return _ffn(x1, mod3, row2(norm2_g[0]), w_up_b[:, :D_FF], w_up_b[:, D_FF:], conv_w[0].reshape(9, D_FF),
                row2(conv_b[0]), w_down[0].astype(BF16), row2(final_g), tm=512)
```

```python
import functools

import numpy as np
import jax
import jax.numpy as jnp
from jax import lax
from jax.experimental import pallas as pl
from jax.experimental.pallas import tpu as pltpu

F32 = jnp.float32
BF16 = jnp.bfloat16

D_MODEL = 1024
GRID_W = 64
EPS = 1e-6
MLA_HEADS = 8
MLA_NOPE = 64
MLA_ROPE = 32
MLA_V_HEAD = 64
MLA_Q_LORA = 384
MLA_KV_LORA = 256
MLA_WIDTH = MLA_HEADS * MLA_V_HEAD
MLA_SCALE = (MLA_NOPE + MLA_ROPE) ** -0.5
ROPE_AXIS_DIM = MLA_ROPE // 2
ROPE_THETA = 10000.0
GLA_HEADS = 4
GLA_HEAD_K = 128
GLA_HEAD_V = 128
GLA_KEY = GLA_HEADS * GLA_HEAD_K
GLA_VALUE = GLA_HEADS * GLA_HEAD_V
GLA_QSCALE = GLA_HEAD_K ** -0.5
GATE_RANK = 16
GATE_NORM = 16.0
GLA_CHUNK = 64
D_FF = 2816
LOG2E = 1.4426950408889634
ONES_ROWS = 16
HEAD_PAD = 128

VMEM_LIMIT = 56 * 1024 * 1024

A_Q = (0, 384)
A_MISC = (384, 512)
A_KV = (512, 768)
A_GQ = (768, 1280)
A_GK = (1280, 1792)
A_GV = (1792, 2304)
A_COLS = 2304


def _dot(a, b):
    return jnp.dot(a, b, preferred_element_type=F32)


def _dot_nt(a, b):
    return lax.dot_general(a, b, (((1,), (1,)), ((), ())), preferred_element_type=F32)


def _dot_tn(a, b):
    return lax.dot_general(a, b, (((0,), (0,)), ((), ())), preferred_element_type=F32)


def _rms(x, g):
    return x * lax.rsqrt(jnp.mean(x * x, axis=-1, keepdims=True) + EPS) * g


def _const_spec(shape):
    nd = len(shape)
    return pl.BlockSpec(shape, lambda *_: (0,) * nd)


def _params(n_grid):
    return pltpu.CompilerParams(dimension_semantics=("arbitrary",) * n_grid, vmem_limit_bytes=VMEM_LIMIT)


def _ada_kernel(c_ref, w_ref, b_ref, o_ref):
    c = c_ref[...]
    s = (c * jax.nn.sigmoid(c)).astype(BF16)
    o_ref[...] = _dot(s, w_ref[...].astype(BF16)) + b_ref[...]


def _ada(c16, w_ada, b_ada):
    n = w_ada.shape[1]
    bn = 512
    return pl.pallas_call(
        _ada_kernel,
        grid=(n // bn,),
        in_specs=[_const_spec((16, D_MODEL)),
                  pl.BlockSpec((D_MODEL, bn), lambda j: (0, j)),
                  pl.BlockSpec((1, bn), lambda j: (0, j))],
        out_specs=pl.BlockSpec((16, bn), lambda j: (0, j)),
        out_shape=jax.ShapeDtypeStruct((16, n), F32),
        compiler_params=_params(1),
        name="ada",
    )(c16, w_ada, b_ada.reshape(1, n))


def _inproj_kernel(x_ref, sh_ref, sc_ref, g1_ref, tab_ref, w_ref, qg_ref, wuq_ref, kvg_ref, wk_ref, wvt_ref,
                   wd_ref, bd_ref, *out_refs, with_q):
    if with_q:
        q_ref, k_ref, v_ref, gq_ref, gk_ref, gv_ref, g_ref = out_refs
    else:
        k_ref, v_ref, gk_ref, gv_ref, g_ref = out_refs
    x = x_ref[0]
    h = (_rms(x, g1_ref[...]) * (1.0 + sc_ref[0]) + sh_ref[0]).astype(BF16)

    if with_q:
        zqm = _dot(h, w_ref[:, A_Q[0]:A_MISC[1]])
        misc = zqm[:, A_MISC[0]:]
        qn = _rms(zqm[:, :A_Q[1]], qg_ref[...]).astype(BF16)
        qf = _dot(qn, wuq_ref[...])
        tq = tab_ref[:, 0:HEAD_PAD]
        for hh in range(MLA_HEADS):
            sl = slice(hh * HEAD_PAD, (hh + 1) * HEAD_PAD)
            q_ref[0, :, sl] = (qf[:, sl] * tq).astype(BF16)
        gq_ref[0] = (_dot(h, w_ref[:, A_GQ[0]:A_GQ[1]]) * GLA_QSCALE).astype(BF16)

    zkv = _dot(h, w_ref[:, A_KV[0]:A_KV[1]])
    kvn = _rms(zkv, kvg_ref[...]).astype(BF16)
    if not with_q:
        misc = _dot(h, w_ref[:, A_MISC[0]:A_MISC[1]])
    kr = (misc * tab_ref[:, 128:256]
          + pltpu.roll(misc, HEAD_PAD - 8, 1) * tab_ref[:, 256:384]
          + pltpu.roll(misc, 8, 1) * tab_ref[:, 384:512])
    k_ref[0] = _dot(jnp.concatenate([kvn, kr.astype(BF16)], axis=1), wk_ref[...]).astype(BF16)
    v_ref[0, 0] = _dot_nt(wvt_ref[...], kvn).astype(BF16)

    gk_ref[0] = _dot(h, w_ref[:, A_GK[0]:A_GK[1]]).astype(BF16)
    gv_ref[0] = _dot(h, w_ref[:, A_GV[0]:A_GV[1]]).astype(BF16)

    gp = _dot(misc.astype(BF16), wd_ref[...]) + bd_ref[...]
    g_ref[0] = (jnp.minimum(gp, 0.0) - jnp.log1p(jnp.exp(-jnp.abs(gp)))) * (1.0 / GATE_NORM)


def _inproj(x, mod3, mod_row, g1, tab, w_a, qg, wuq, kvg, wk, wvt, wd, bd, *, tm, with_q):
    b, l, _ = x.shape
    tm = min(tm, l)
    nt = l // tm
    row = (lambda bi: bi) if mod_row is None else (lambda bi: mod_row)
    tok = lambda w: pl.BlockSpec((1, tm, w), lambda bi, i: (bi, i, 0))
    in_specs = [
        tok(D_MODEL),
        pl.BlockSpec((1, 1, D_MODEL), lambda bi, i: (row(bi), 0, 0)),
        pl.BlockSpec((1, 1, D_MODEL), lambda bi, i: (row(bi), 0, 1)),
        _const_spec((1, D_MODEL)),
        pl.BlockSpec((tm, 512), lambda bi, i: (i, 0)),
        _const_spec(w_a.shape), _const_spec(qg.shape), _const_spec(wuq.shape), _const_spec(kvg.shape),
        _const_spec(wk.shape), _const_spec(wvt.shape), _const_spec(wd.shape), _const_spec(bd.shape),
    ]
    sds = lambda w, dt: jax.ShapeDtypeStruct((b, l, w), dt)
    kw = MLA_HEADS * HEAD_PAD
    outs = [(kw, BF16), None, (GLA_KEY, BF16), (GLA_VALUE, BF16), (2 * GLA_KEY, F32)]
    if with_q:
        outs = [(kw, BF16)] + outs[:2] + [(GLA_KEY, BF16)] + outs[2:]
    vt_spec = pl.BlockSpec((1, 1, MLA_WIDTH, tm), lambda bi, i: (bi, i, 0, 0))
    vt_shape = jax.ShapeDtypeStruct((b, nt, MLA_WIDTH, tm), BF16)
    return pl.pallas_call(
        functools.partial(_inproj_kernel, with_q=with_q),
        grid=(b, nt),
        in_specs=in_specs,
        out_specs=[vt_spec if o is None else tok(o[0]) for o in outs],
        out_shape=[vt_shape if o is None else sds(*o) for o in outs],
        compiler_params=_params(2),
        name="inproj_lat" if with_q else "inproj_ctx",
    )(x, mod3, mod3, g1, tab, w_a, qg, wuq, kvg, wk, wvt, wd, bd)


def _attn_kernel(q_ref, kl_ref, kc_ref, vl_ref, vc_ref, o_ref, sa_ref, sb_ref, sc_ref):
    tq = q_ref.shape[1]
    n_lat, _, tk = vl_ref.shape[1:]
    assert n_lat >= 2 and n_lat % 2 == 0

    def scores(dst, k_t):
        for hh in range(2):
            cols = slice(hh * HEAD_PAD, (hh + 1) * HEAD_PAD)
            dst[hh] = _dot_nt(k_t[:, cols], q_ref[0, :, cols])

    def consume(src, vt, carry):
        ones = jnp.ones((ONES_ROWS, vt.shape[1]), BF16)
        out = []
        for hh in range(2):
            m, acc = carry[hh]
            s = src[hh]
            m_new = jnp.maximum(m, jnp.max(s, axis=0, keepdims=True))
            p = jnp.exp2(s - m_new).astype(BF16)
            v_aug = jnp.concatenate([vt[hh * MLA_V_HEAD:(hh + 1) * MLA_V_HEAD], ones], axis=0)
            out.append((m_new, acc * jnp.exp2(m - m_new) + _dot(v_aug, p)))
        return tuple(out)

    def k_tile(i):
        return kl_ref[0, pl.ds(pl.multiple_of(i * tk, tk), tk), :]

    def pair(it, carry):
        j = 2 * it
        scores(sb_ref, k_tile(j + 1))
        carry = consume(sa_ref, vl_ref[0, j], carry)
        scores(sa_ref, k_tile(j + 2))
        return consume(sb_ref, vl_ref[0, j + 1], carry)

    head0 = (jnp.full((1, tq), -jnp.inf, F32), jnp.zeros((MLA_V_HEAD + ONES_ROWS, tq), F32))
    scores(sa_ref, k_tile(0))
    carry = lax.fori_loop(0, (n_lat - 2) // 2, pair, (head0, head0))
    scores(sb_ref, k_tile(n_lat - 1))
    carry = consume(sa_ref, vl_ref[0, n_lat - 2], carry)
    scores(sc_ref, kc_ref[0])
    carry = consume(sb_ref, vl_ref[0, n_lat - 1], carry)
    (_, acc0), (_, acc1) = consume(sc_ref, vc_ref[0, 0], carry)
    ot = jnp.concatenate([acc[:MLA_V_HEAD] / acc[MLA_V_HEAD:MLA_V_HEAD + 1] for acc in (acc0, acc1)], axis=0)
    o_ref[0] = ot.T.astype(o_ref.dtype)


def _attention(q, k_lat, k_ctx, vt_lat, vt_ctx, *, tq):
    b, l, _ = q.shape
    c = k_ctx.shape[1]
    n_lat, _, tk = vt_lat.shape[1:]
    pairs = MLA_HEADS // 2
    return pl.pallas_call(
        _attn_kernel,
        grid=(b, pairs, l // tq),
        in_specs=[
            pl.BlockSpec((1, tq, 2 * HEAD_PAD), lambda bi, p, i: (bi, i, p)),
            pl.BlockSpec((1, l, 2 * HEAD_PAD), lambda bi, p, i: (bi, 0, p)),
            pl.BlockSpec((1, c, 2 * HEAD_PAD), lambda bi, p, i: (bi, 0, p)),
            pl.BlockSpec((1, n_lat, 2 * MLA_V_HEAD, tk), lambda bi, p, i: (bi, 0, p, 0)),
            pl.BlockSpec((1, 1, 2 * MLA_V_HEAD, c), lambda bi, p, i: (bi, 0, p, 0)),
        ],
        out_specs=pl.BlockSpec((1, tq, 2 * MLA_V_HEAD), lambda bi, p, i: (bi, i, p)),
        out_shape=jax.ShapeDtypeStruct((b, l, MLA_WIDTH), BF16),
        scratch_shapes=[pltpu.VMEM((2, tk, tq), F32), pltpu.VMEM((2, tk, tq), F32), pltpu.VMEM((2, c, tq), F32)],
        compiler_params=_params(3),
        name="attn",
    )(q, k_lat, k_ctx, vt_lat, vt_ctx)


def _gla_kernel(qf_ref, kf_ref, vf_ref, gf_ref, qb_ref, kb_ref, vb_ref, gb_ref, s0_ref,
                of_ref, ob_ref, sout_ref, st_scr, b_scr):
    i = pl.program_id(1)
    t = qf_ref.shape[1]
    n_chunks = t // GLA_CHUNK
    grp = min(t, 256)
    dirs = ((qf_ref, kf_ref, vf_ref, gf_ref, of_ref), (qb_ref, kb_ref, vb_ref, gb_ref, ob_ref))

    @pl.when(i == 0)
    def _():
        st_scr[...] = s0_ref[:, 0]

    r = lax.broadcasted_iota(jnp.int32, (grp, grp), 0)
    c = lax.broadcasted_iota(jnp.int32, (grp, grp), 1)
    same = (r // GLA_CHUNK) == (c // GLA_CHUNK)
    for d, refs in enumerate(dirs):
        tri = jnp.where(same & ((c >= r) if d else (c <= r)), 1.0, 0.0).astype(BF16)
        for gi in range(t // grp):
            g = refs[3][0, gi * grp:(gi + 1) * grp, :]
            hi = g.astype(BF16)
            r1 = g - hi.astype(F32)
            mid = r1.astype(BF16)
            lo = (r1 - mid.astype(F32)).astype(BF16)
            b_scr[d, gi * grp:(gi + 1) * grp, :] = _dot(tri, hi) + _dot(tri, mid) + _dot(tri, lo)

    rr = lax.broadcasted_iota(jnp.int32, (GLA_CHUNK, GLA_CHUNK), 0)
    cc = lax.broadcasted_iota(jnp.int32, (GLA_CHUNK, GLA_CHUNK), 1)

    def chunk(j, _):
        for d, (q_ref, k_ref, v_ref, _, o_ref) in enumerate(dirs):
            n = (n_chunks - 1 - j) if d else j
            keep = (cc >= rr) if d else (cc <= rr)
            edge = 0 if d else GLA_CHUNK - 1
            rows = pl.ds(pl.multiple_of(n * GLA_CHUNK, GLA_CHUNK), GLA_CHUNK)
            for hh in range(GLA_HEADS):
                cols = slice(hh * GLA_HEAD_K, (hh + 1) * GLA_HEAD_K)
                bc = b_scr[d, rows, cols]
                b_all = bc[edge:edge + 1, :]
                qc = q_ref[0, rows, cols].astype(F32)
                kc = k_ref[0, rows, cols].astype(F32)
                vc = v_ref[0, rows, cols]
                qe = (qc * jnp.exp(bc)).astype(BF16)
                ke = (kc * jnp.exp(-bc)).astype(BF16)
                kd = (kc * jnp.exp(b_all - bc)).astype(BF16)
                st = st_scr[d, hh]
                a = jnp.where(keep, _dot_nt(qe, ke), 0.0).astype(BF16)
                o = _dot(a, vc) + _dot_nt(qe, st.astype(BF16))
                o_ref[0, rows, cols] = o.astype(o_ref.dtype)
                st_scr[d, hh] = st * jnp.exp(b_all) + _dot_tn(vc, kd)
        return 0

    lax.fori_loop(0, n_chunks, chunk, 0, unroll=2)

    @pl.when(i == pl.num_programs(1) - 1)
    def _():
        sout_ref[:, 0] = st_scr[...]


def _gla(q, k, v, g, s0, *, tile):
    b, l, _ = q.shape
    tile = min(tile, l)
    nt = l // tile
    fwd = pl.BlockSpec((1, tile, GLA_KEY), lambda bi, i: (bi, i, 0))
    bwd = pl.BlockSpec((1, tile, GLA_KEY), lambda bi, i: (bi, nt - 1 - i, 0))
    st_spec = pl.BlockSpec((2, 1, GLA_HEADS, GLA_HEAD_V, GLA_HEAD_K), lambda bi, i: (0, bi, 0, 0, 0))
    return pl.pallas_call(
        _gla_kernel,
        grid=(b, nt),
        in_specs=[fwd, fwd, fwd, fwd, bwd, bwd, bwd,
                  pl.BlockSpec((1, tile, GLA_KEY), lambda bi, i: (bi, nt - 1 - i, 1)),
                  st_spec],
        out_specs=[fwd, bwd, st_spec],
        out_shape=[jax.ShapeDtypeStruct((b, l, GLA_VALUE), BF16), jax.ShapeDtypeStruct((b, l, GLA_VALUE), BF16),
                   jax.ShapeDtypeStruct((2, b, GLA_HEADS, GLA_HEAD_V, GLA_HEAD_K), F32)],
        scratch_shapes=[pltpu.VMEM((2, GLA_HEADS, GLA_HEAD_V, GLA_HEAD_K), F32),
                        pltpu.VMEM((2, tile, GLA_KEY), F32)],
        compiler_params=_params(2),
        name="gla",
    )(q, k, v, g, q, k, v, g, s0)


def _mix_kernel(x_ref, sh_ref, sc_ref, gt_ref, g1_ref, om_ref, of_ref, ob_ref, wb_ref, gng_ref,
                wbm_ref, wbg_ref, wo_ref, o_ref):
    x = x_ref[0]
    h = (_rms(x, g1_ref[...]) * (1.0 + sc_ref[0]) + sh_ref[0]).astype(BF16)
    zr = _dot(h, wb_ref[:, 0:GLA_VALUE])
    og = of_ref[0].astype(F32) + ob_ref[0].astype(F32)
    ys = []
    for hh in range(GLA_HEADS):
        cols = slice(hh * GLA_HEAD_V, (hh + 1) * GLA_HEAD_V)
        r = zr[:, cols]
        ys.append((_rms(og[:, cols], gng_ref[...]) * (r * jax.nn.sigmoid(r))).astype(BF16))
    br_gla = _dot(jnp.concatenate(ys, axis=1), wbg_ref[...])
    br_mla = _dot(om_ref[0], wbm_ref[...])
    g_mla = jax.nn.sigmoid(_dot(h, wb_ref[:, GLA_VALUE:GLA_VALUE + D_MODEL]))
    g_gla = jax.nn.sigmoid(_dot(h, wb_ref[:, GLA_VALUE + D_MODEL:]))
    merged = (g_mla * br_mla + g_gla * br_gla).astype(BF16)
    o_ref[0] = x + gt_ref[0] * _dot(merged, wo_ref[...])


def _mix(x, mod3, g1, o_mla, o_f, o_b, w_b, gng, wbm, wbg, wo, *, tm):
    b, l, _ = x.shape
    tok = lambda w: pl.BlockSpec((1, tm, w), lambda bi, i: (bi, i, 0))
    modc = lambda j: pl.BlockSpec((1, 1, D_MODEL), lambda bi, i: (bi, 0, j))
    return pl.pallas_call(
        _mix_kernel,
        grid=(b, l // tm),
        in_specs=[tok(D_MODEL), modc(0), modc(1), modc(2), _const_spec((1, D_MODEL)),
                  tok(MLA_WIDTH), tok(GLA_VALUE), tok(GLA_VALUE),
                  _const_spec(w_b.shape), _const_spec(gng.shape), _const_spec(wbm.shape),
                  _const_spec(wbg.shape), _const_spec(wo.shape)],
        out_specs=tok(D_MODEL),
        out_shape=jax.ShapeDtypeStruct((b, l, D_MODEL), F32),
        compiler_params=_params(2),
        name="mix",
    )(x, mod3, mod3, mod3, g1, o_mla, o_f, o_b, w_b, gng, wbm, wbg, wo)


FF_CHUNK = 256


def _ffn_kernel(xm_ref, xp_ref, xn_ref, sh_ref, sc_ref, gt_ref, g2_ref, wv_ref, wg_ref, cw_ref, cb_ref,
                wd_ref, fg_ref, o_ref, he_scr, gate_scr, val_scr, act_scr, acc_scr):
    i = pl.program_id(1)
    tm = xm_ref.shape[1]
    te = tm + 2 * GRID_W
    n_chunks = D_FF // FF_CHUNK

    def normed(x, keep=None):
        h = _rms(x, g2_ref[...]) * (1.0 + sc_ref[0]) + sh_ref[0]
        return (h if keep is None else h * keep).astype(BF16)

    keep_prev = jnp.where(i > 0, 1.0, 0.0).astype(F32)
    keep_next = jnp.where(i < pl.num_programs(1) - 1, 1.0, 0.0).astype(F32)
    he_scr[0:GRID_W] = normed(xp_ref[0], keep_prev)
    he_scr[GRID_W:GRID_W + tm] = normed(xm_ref[0])
    he_scr[GRID_W + tm:te] = normed(xn_ref[0], keep_next)

    col = lax.broadcasted_iota(jnp.int32, (GRID_W, 1), 0)
    has_left = col > 0
    has_right = col < GRID_W - 1

    def project(c):
        cs = slice(c * FF_CHUNK, (c + 1) * FF_CHUNK)
        gate_scr[c % 2] = _dot(he_scr[...], wg_ref[:, cs])
        val_scr[c % 2] = _dot(he_scr[GRID_W:GRID_W + tm], wv_ref[:, cs])

    def activate(c):
        slot = c % 2
        for j in range(tm // GRID_W):
            for lh in range(FF_CHUNK // 128):
                lanes = slice(lh * 128, (lh + 1) * 128)
                wl = slice(c * FF_CHUNK + lh * 128, c * FF_CHUNK + (lh + 1) * 128)
                sums = [None, None, None]
                for dr in range(3):
                    blk = gate_scr[slot, (j + dr) * GRID_W:(j + dr + 1) * GRID_W, lanes]
                    for dc in range(3):
                        term = blk * cw_ref[3 * dr + dc:3 * dr + dc + 1, wl]
                        sums[dc] = term if sums[dc] is None else sums[dc] + term
                conv = (cb_ref[:, wl] + sums[1]
                        + jnp.where(has_left, pltpu.roll(sums[0], 1, 0), 0.0)
                        + jnp.where(has_right, pltpu.roll(sums[2], GRID_W - 1, 0), 0.0))
                gelu = 0.5 * conv * (1.0 + lax.erf(conv * np.float32(2.0 ** -0.5)))
                rows = slice(j * GRID_W, (j + 1) * GRID_W)
                act_scr[slot, rows, lanes] = (gelu * val_scr[slot, rows, lanes]).astype(BF16)

    def down(c):
        part = _dot(act_scr[c % 2], wd_ref[c * FF_CHUNK:(c + 1) * FF_CHUNK, :])
        if c == 0:
            acc_scr[...] = part
        else:
            acc_scr[...] += part

    project(0)
    for s in range(n_chunks):
        if s + 1 < n_chunks:
            project(s + 1)
        activate(s)
        down(s)
    x2 = xm_ref[0] + gt_ref[0] * acc_scr[...]
    o_ref[0] = _rms(x2, fg_ref[...])


def _ffn(x1, mod3, g2, w_up, cw, cb, wd, fg, *, tm):
    b, l, _ = x1.shape
    half = lambda j: pl.BlockSpec((D_MODEL, D_FF), lambda bi, i: (0, j))
    rpt = tm // GRID_W
    n_rows = l // GRID_W
    tok = lambda w: pl.BlockSpec((1, tm, w), lambda bi, i: (bi, i, 0))
    modc = lambda j: pl.BlockSpec((1, 1, D_MODEL), lambda bi, i: (bi, 0, j))
    return pl.pallas_call(
        _ffn_kernel,
        grid=(b, l // tm),
        in_specs=[tok(D_MODEL),
                  pl.BlockSpec((1, GRID_W, D_MODEL), lambda bi, i: (bi, jnp.maximum(i * rpt - 1, 0), 0)),
                  pl.BlockSpec((1, GRID_W, D_MODEL),
                               lambda bi, i: (bi, jnp.minimum((i + 1) * rpt, n_rows - 1), 0)),
                  modc(3), modc(4), modc(5), _const_spec((1, D_MODEL)),
                  half(0), half(1), _const_spec(cw.shape), _const_spec(cb.shape),
                  _const_spec(wd.shape), _const_spec((1, D_MODEL))],
        out_specs=tok(D_MODEL),
        out_shape=jax.ShapeDtypeStruct((b, l, D_MODEL), F32),
        scratch_shapes=[pltpu.VMEM((tm + 2 * GRID_W, D_MODEL), BF16),
                        pltpu.VMEM((2, tm + 2 * GRID_W, FF_CHUNK), F32),
                        pltpu.VMEM((2, tm, FF_CHUNK), F32),
                        pltpu.VMEM((2, tm, FF_CHUNK), BF16),
                        pltpu.VMEM((tm, D_MODEL), F32)],
        compiler_params=_params(2),
        name="ffn",
    )(x1, x1, x1, mod3, mod3, mod3, g2, w_up, w_up, cw, cb, wd, fg)


def _rope_tables(length):
    f32 = np.float32
    t = np.arange(length)
    inv_freq = (f32(ROPE_THETA) ** (-np.arange(0, ROPE_AXIS_DIM, 2, dtype=f32) / f32(ROPE_AXIS_DIM))).astype(f32)
    row = (t // GRID_W).astype(f32)[:, None] * inv_freq
    col = (t % GRID_W).astype(f32)[:, None] * inv_freq
    ang = np.concatenate([row, row, col, col], axis=1)
    cos, sin = np.cos(ang).astype(f32), np.sin(ang).astype(f32)
    z8 = np.zeros((length, ROPE_AXIS_DIM // 2), f32)
    pad = np.zeros((length, HEAD_PAD - MLA_ROPE), f32)
    tq = f32(MLA_SCALE * LOG2E) * np.concatenate([np.ones((length, MLA_NOPE), f32), cos, sin], axis=1)
    s_row, s_col = np.sin(row).astype(f32), np.sin(col).astype(f32)
    s_next = np.concatenate([-s_row, z8, -s_col, z8, pad], axis=1)
    s_prev = np.concatenate([z8, s_row, z8, s_col, pad], axis=1)
    return jnp.asarray(np.concatenate([tq, np.concatenate([cos, pad], axis=1), s_next, s_prev], axis=1))


def _ctx_tables(length):
    one = np.concatenate([np.ones((length, MLA_ROPE), np.float32),
                          np.zeros((length, HEAD_PAD - MLA_ROPE), np.float32)], 1)
    z = np.zeros((length, HEAD_PAD), np.float32)
    return jnp.asarray(np.concatenate([z, one, z, z], axis=1))


def _rot_partner(w):
    h = ROPE_AXIS_DIM // 2
    r1, r2, c1, c2 = w[..., 0:h], w[..., h:2 * h], w[..., 2 * h:3 * h], w[..., 3 * h:4 * h]
    return jnp.concatenate([-r2, r1, -c2, c1], axis=-1)


def _layout_weights(w_in, w_uq, w_ukv, w_decay, b_decay):
    offs = np.cumsum((0, MLA_Q_LORA, MLA_KV_LORA, MLA_ROPE, GLA_KEY, GLA_KEY, GLA_VALUE, GLA_VALUE,
                      2 * GATE_RANK, 2 * D_MODEL))
    part = lambda j: w_in[:, offs[j]:offs[j + 1]]
    zc = lambda n: jnp.zeros((D_MODEL, n), F32)
    w_a = jnp.concatenate([part(0), part(2), part(7), zc(HEAD_PAD - MLA_ROPE - 2 * GATE_RANK), part(1),
                           part(3), part(4), part(5)], axis=1).astype(BF16)
    w_b = jnp.concatenate([part(6), part(8)], axis=1).astype(BF16)

    uq = w_uq.reshape(MLA_Q_LORA, MLA_HEADS, MLA_NOPE + MLA_ROPE)
    rope = uq[..., MLA_NOPE:]
    wuq = jnp.concatenate([uq, _rot_partner(rope)], axis=-1).reshape(MLA_Q_LORA, MLA_HEADS * HEAD_PAD)

    ukv = w_ukv.reshape(MLA_KV_LORA, MLA_HEADS, MLA_NOPE + MLA_V_HEAD)
    k_cols = jnp.concatenate([ukv[..., :MLA_NOPE], jnp.zeros((MLA_KV_LORA, MLA_HEADS, HEAD_PAD - MLA_NOPE), F32)],
                             axis=-1).reshape(MLA_KV_LORA, MLA_HEADS * HEAD_PAD)
    v_cols = ukv[..., MLA_NOPE:].reshape(MLA_KV_LORA, MLA_WIDTH)
    eye = jnp.eye(HEAD_PAD, MLA_ROPE, dtype=F32)
    route = jnp.concatenate([jnp.zeros((HEAD_PAD, MLA_NOPE), F32), eye, eye], axis=1)
    route = jnp.tile(route, (1, MLA_HEADS))
    wk = jnp.concatenate([k_cols, route], axis=0)

    wd = jnp.zeros((HEAD_PAD, 2 * GLA_KEY), F32)
    wd = wd.at[MLA_ROPE:MLA_ROPE + GATE_RANK, :GLA_KEY].set(w_decay[0])
    wd = wd.at[MLA_ROPE + GATE_RANK:MLA_ROPE + 2 * GATE_RANK, GLA_KEY:].set(w_decay[1])
    bd = b_decay.reshape(1, 2 * GLA_KEY)
    return w_a, w_b, wuq.astype(BF16), wk.astype(BF16), v_cols.T.astype(BF16), wd.astype(BF16), bd


def kernel(x, c, ctx, c_ctx, w_ada, b_ada, norm1_g, w_in, q_norm_g, w_uq, kv_norm_g, w_ukv, gla_w_decay,
           gla_b_decay, gla_norm_g, w_br_mla, w_br_gla, w_out, norm2_g, w_up, conv_w, conv_b, w_down, final_g):
    b, l, _ = x.shape
    n_ctx = ctx.shape[1]
    assert w_ada.shape[0] == 1, "single layer: context tokens are never updated"
    row2 = lambda a: a.reshape(1, -1)

    c16 = jnp.concatenate([c, c_ctx[None], jnp.zeros((16 - b - 1, D_MODEL), F32)], axis=0)
    mod3 = _ada(c16, w_ada[0], b_ada[0]).reshape(16, 1, 6 * D_MODEL)

    w_a, w_b, wuq, wk, wvt, wd, bd = _layout_weights(w_in[0], w_uq[0], w_ukv[0], gla_w_decay[0], gla_b_decay[0])
    g1 = row2(norm1_g[0])
    proj = functools.partial(_inproj, g1=g1, w_a=w_a, qg=row2(q_norm_g[0]), wuq=wuq, kvg=row2(kv_norm_g[0]),
                             wk=wk, wvt=wvt, wd=wd, bd=bd, tm=512)
    q, k, v, gq, gk, gv, g = proj(x, mod3, None, tab=_rope_tables(l), with_q=True)
    k_c, v_c, gk_c, gv_c, g_c = proj(ctx, mod3, b, tab=_ctx_tables(n_ctx), with_q=False)

    o_mla = _attention(q, k, k_c, v, v_c, tq=1024)

    s0 = jnp.zeros((2, b, GLA_HEADS, GLA_HEAD_V, GLA_HEAD_K), F32)
    zq = jnp.zeros((b, n_ctx, GLA_KEY), BF16)
    _, _, s_ctx = _gla(zq, gk_c, gv_c, g_c, s0, tile=256)
    o_f, o_b, _ = _gla(gq, gk, gv, g, s_ctx, tile=512)

    x1 = _mix(x, mod3, g1, o_mla, o_f, o_b, w_b, row2(gla_norm_g[0]), w_br_mla[0].astype(BF16),
              w_br_gla[0].astype(BF16), w_out[0].astype(BF16), tm=512)

    return _ffn(x1, mod3, row2(norm2_g[0]), w_up[0].astype(BF16), conv_w[0].reshape(9, D_FF),
                row2(conv_b[0]), w_down[0].astype(BF16), row2(final_g), tm=512)
```

```python
import functools

import numpy as np
import jax
import jax.numpy as jnp
from jax import lax
from jax.experimental import pallas as pl
from jax.experimental.pallas import tpu as pltpu

F32 = jnp.float32
BF16 = jnp.bfloat16

D_MODEL = 1024
GRID_W = 64
EPS = 1e-6
MLA_HEADS = 8
MLA_NOPE = 64
MLA_ROPE = 32
MLA_V_HEAD = 64
MLA_Q_LORA = 384
MLA_KV_LORA = 256
MLA_WIDTH = MLA_HEADS * MLA_V_HEAD
MLA_SCALE = (MLA_NOPE + MLA_ROPE) ** -0.5
ROPE_AXIS_DIM = MLA_ROPE // 2
ROPE_THETA = 10000.0
GLA_HEADS = 4
GLA_HEAD_K = 128
GLA_HEAD_V = 128
GLA_KEY = GLA_HEADS * GLA_HEAD_K
GLA_VALUE = GLA_HEADS * GLA_HEAD_V
GLA_QSCALE = GLA_HEAD_K ** -0.5
GATE_RANK = 16
GATE_NORM = 16.0
GLA_CHUNK = 64
D_FF = 2816
LOG2E = 1.4426950408889634
TOKEN_TILE = 1024
FFN_TILE = 512
ATT_KEY_TILE = 512
ONES_ROWS = 16
HEAD_PAD = 128

VMEM_LIMIT = 56 * 1024 * 1024

A_Q = (0, 384)
A_MISC = (384, 512)
A_KV = (512, 768)
A_GQ = (768, 1280)
A_GK = (1280, 1792)
A_GV = (1792, 2304)
A_COLS = 2304


def _dot(a, b):
    return jnp.dot(a, b, preferred_element_type=F32)


def _dot_nt(a, b):
    return lax.dot_general(a, b, (((1,), (1,)), ((), ())), preferred_element_type=F32)


def _dot_tn(a, b):
    return lax.dot_general(a, b, (((0,), (0,)), ((), ())), preferred_element_type=F32)


def _rms(x, g):
    return x * lax.rsqrt(jnp.mean(x * x, axis=-1, keepdims=True) + EPS) * g


def _const_spec(shape):
    nd = len(shape)
    return pl.BlockSpec(shape, lambda *_: (0,) * nd, pipeline_mode=pl.Buffered(1))


def _params(n_grid):
    return pltpu.CompilerParams(dimension_semantics=("arbitrary",) * n_grid, vmem_limit_bytes=VMEM_LIMIT)


def _ada_kernel(c_ref, w_ref, b_ref, o_ref):
    c = c_ref[...]
    s = (c * jax.nn.sigmoid(c)).astype(BF16)
    o_ref[...] = _dot(s, w_ref[...].astype(BF16)) + b_ref[...]


def _ada(c16, w_ada, b_ada):
    n = w_ada.shape[1]
    bn = 512
    return pl.pallas_call(
        _ada_kernel,
        grid=(n // bn,),
        in_specs=[_const_spec((16, D_MODEL)),
                  pl.BlockSpec((D_MODEL, bn), lambda j: (0, j)),
                  pl.BlockSpec((1, bn), lambda j: (0, j))],
        out_specs=pl.BlockSpec((16, bn), lambda j: (0, j)),
        out_shape=jax.ShapeDtypeStruct((16, n), F32),
        compiler_params=_params(1),
        name="ada",
    )(c16, w_ada, b_ada.reshape(1, n))


def _inproj_kernel(x_ref, sh_ref, sc_ref, g1_ref, tab_ref, w_ref, qg_ref, wuq_ref, kvg_ref, wk_ref, wvt_ref,
                   wd_ref, bd_ref, *out_refs, with_q):
    if with_q:
        q_ref, k_ref, v_ref, gq_ref, gk_ref, gv_ref, g_ref = out_refs
    else:
        k_ref, v_ref, gk_ref, gv_ref, g_ref = out_refs
    x = x_ref[0]
    h = (_rms(x, g1_ref[...]) * (1.0 + sc_ref[0]) + sh_ref[0]).astype(BF16)

    if with_q:
        zqm = _dot(h, w_ref[:, A_Q[0]:A_MISC[1]])
        misc = zqm[:, A_MISC[0]:]
        qn = _rms(zqm[:, :A_Q[1]], qg_ref[...]).astype(BF16)
        qf = _dot(qn, wuq_ref[...])
        tq = tab_ref[:, 0:HEAD_PAD]
        for hh in range(MLA_HEADS):
            sl = slice(hh * HEAD_PAD, (hh + 1) * HEAD_PAD)
            q_ref[0, :, sl] = (qf[:, sl] * tq).astype(BF16)
        gq_ref[0] = (_dot(h, w_ref[:, A_GQ[0]:A_GQ[1]]) * GLA_QSCALE).astype(BF16)

    zkv = _dot(h, w_ref[:, A_KV[0]:A_KV[1]])
    kvn = _rms(zkv, kvg_ref[...]).astype(BF16)
    if not with_q:
        misc = _dot(h, w_ref[:, A_MISC[0]:A_MISC[1]])
    kr = (misc * tab_ref[:, 128:256]
          + pltpu.roll(misc, HEAD_PAD - 8, 1) * tab_ref[:, 256:384]
          + pltpu.roll(misc, 8, 1) * tab_ref[:, 384:512])
    k_ref[0] = _dot(jnp.concatenate([kvn, kr.astype(BF16)], axis=1), wk_ref[...]).astype(BF16)
    vt = _dot_nt(wvt_ref[...], kvn).astype(BF16)
    kt = v_ref.shape[3]
    for sub in range(v_ref.shape[1]):
        v_ref[0, sub] = vt[:, sub * kt:(sub + 1) * kt]

    gk_ref[0] = _dot(h, w_ref[:, A_GK[0]:A_GK[1]]).astype(BF16)
    gv_ref[0] = _dot(h, w_ref[:, A_GV[0]:A_GV[1]]).astype(BF16)

    gp = _dot(misc.astype(BF16), wd_ref[...]) + bd_ref[...]
    g_ref[0] = (jnp.minimum(gp, 0.0) - jnp.log1p(jnp.exp(-jnp.abs(gp)))) * (1.0 / GATE_NORM)


def _inproj(x, mod3, mod_row, g1, tab, w_a, qg, wuq, kvg, wk, wvt, wd, bd, *, tm, with_q):
    b, l, _ = x.shape
    tm = min(tm, l)
    nt = l // tm
    row = (lambda bi: bi) if mod_row is None else (lambda bi: mod_row)
    tok = lambda w: pl.BlockSpec((1, tm, w), lambda bi, i: (bi, i, 0))
    in_specs = [
        tok(D_MODEL),
        pl.BlockSpec((1, 1, D_MODEL), lambda bi, i: (row(bi), 0, 0)),
        pl.BlockSpec((1, 1, D_MODEL), lambda bi, i: (row(bi), 0, 1)),
        _const_spec((1, D_MODEL)),
        pl.BlockSpec((tm, 512), lambda bi, i: (i, 0)),
        _const_spec(w_a.shape), _const_spec(qg.shape), _const_spec(wuq.shape), _const_spec(kvg.shape),
        _const_spec(wk.shape), _const_spec(wvt.shape), _const_spec(wd.shape), _const_spec(bd.shape),
    ]
    sds = lambda w, dt: jax.ShapeDtypeStruct((b, l, w), dt)
    kw = MLA_HEADS * HEAD_PAD
    outs = [(kw, BF16), None, (GLA_KEY, BF16), (GLA_VALUE, BF16), (2 * GLA_KEY, F32)]
    if with_q:
        outs = [(kw, BF16)] + outs[:2] + [(GLA_KEY, BF16)] + outs[2:]
    kt = min(ATT_KEY_TILE, tm)
    vt_spec = pl.BlockSpec((1, tm // kt, MLA_WIDTH, kt), lambda bi, i: (bi, i, 0, 0))
    vt_shape = jax.ShapeDtypeStruct((b, l // kt, MLA_WIDTH, kt), BF16)
    return pl.pallas_call(
        functools.partial(_inproj_kernel, with_q=with_q),
        grid=(b, nt),
        in_specs=in_specs,
        out_specs=[vt_spec if o is None else tok(o[0]) for o in outs],
        out_shape=[vt_shape if o is None else sds(*o) for o in outs],
        compiler_params=_params(2),
        name="inproj_lat" if with_q else "inproj_ctx",
    )(x, mod3, mod3, g1, tab, w_a, qg, wuq, kvg, wk, wvt, wd, bd)


def _attn_kernel(q_ref, kl_ref, kc_ref, vl_ref, vc_ref, o_ref, sa_ref, sb_ref, sc_ref):
    tq = q_ref.shape[1]
    n_lat, _, tk = vl_ref.shape[1:]
    assert n_lat >= 2 and n_lat % 2 == 0

    def scores(dst, k_t):
        for hh in range(2):
            cols = slice(hh * HEAD_PAD, (hh + 1) * HEAD_PAD)
            dst[hh] = _dot_nt(k_t[:, cols], q_ref[0, :, cols])

    def consume(src, vt, carry):
        ones = jnp.ones((ONES_ROWS, vt.shape[1]), BF16)
        out = []
        for hh in range(2):
            m, acc = carry[hh]
            s = src[hh]
            m_new = jnp.maximum(m, jnp.max(s, axis=0, keepdims=True))
            p = jnp.exp2(s - m_new).astype(BF16)
            v_aug = jnp.concatenate([vt[hh * MLA_V_HEAD:(hh + 1) * MLA_V_HEAD], ones], axis=0)
            out.append((m_new, acc * jnp.exp2(m - m_new) + _dot(v_aug, p)))
        return tuple(out)

    def k_tile(i):
        return kl_ref[0, pl.ds(pl.multiple_of(i * tk, tk), tk), :]

    def pair(it, carry):
        j = 2 * it
        scores(sb_ref, k_tile(j + 1))
        carry = consume(sa_ref, vl_ref[0, j], carry)
        scores(sa_ref, k_tile(j + 2))
        return consume(sb_ref, vl_ref[0, j + 1], carry)

    head0 = (jnp.full((1, tq), -jnp.inf, F32), jnp.zeros((MLA_V_HEAD + ONES_ROWS, tq), F32))
    scores(sa_ref, k_tile(0))
    carry = lax.fori_loop(0, (n_lat - 2) // 2, pair, (head0, head0))
    scores(sb_ref, k_tile(n_lat - 1))
    carry = consume(sa_ref, vl_ref[0, n_lat - 2], carry)
    scores(sc_ref, kc_ref[0])
    carry = consume(sb_ref, vl_ref[0, n_lat - 1], carry)
    (_, acc0), (_, acc1) = consume(sc_ref, vc_ref[0, 0], carry)
    ot = jnp.concatenate([acc[:MLA_V_HEAD] / acc[MLA_V_HEAD:MLA_V_HEAD + 1] for acc in (acc0, acc1)], axis=0)
    o_ref[0] = ot.T.astype(o_ref.dtype)


def _attention(q, k_lat, k_ctx, vt_lat, vt_ctx, *, tq):
    b, l, _ = q.shape
    c = k_ctx.shape[1]
    n_lat, _, tk = vt_lat.shape[1:]
    pairs = MLA_HEADS // 2
    return pl.pallas_call(
        _attn_kernel,
        grid=(b, pairs, l // tq),
        in_specs=[
            pl.BlockSpec((1, tq, 2 * HEAD_PAD), lambda bi, p, i: (bi, i, p)),
            pl.BlockSpec((1, l, 2 * HEAD_PAD), lambda bi, p, i: (bi, 0, p)),
            pl.BlockSpec((1, c, 2 * HEAD_PAD), lambda bi, p, i: (bi, 0, p)),
            pl.BlockSpec((1, n_lat, 2 * MLA_V_HEAD, tk), lambda bi, p, i: (bi, 0, p, 0)),
            pl.BlockSpec((1, 1, 2 * MLA_V_HEAD, c), lambda bi, p, i: (bi, 0, p, 0)),
        ],
        out_specs=pl.BlockSpec((1, tq, 2 * MLA_V_HEAD), lambda bi, p, i: (bi, i, p)),
        out_shape=jax.ShapeDtypeStruct((b, l, MLA_WIDTH), BF16),
        scratch_shapes=[pltpu.VMEM((2, tk, tq), F32), pltpu.VMEM((2, tk, tq), F32), pltpu.VMEM((2, c, tq), F32)],
        compiler_params=_params(3),
        name="attn",
    )(q, k_lat, k_ctx, vt_lat, vt_ctx)


def _gla_kernel(qf_ref, kf_ref, vf_ref, gf_ref, qb_ref, kb_ref, vb_ref, gb_ref, s0_ref,
                of_ref, ob_ref, sout_ref, st_scr, b_scr):
    i = pl.program_id(1)
    t = qf_ref.shape[1]
    n_chunks = t // GLA_CHUNK
    grp = min(t, 256)
    dirs = ((qf_ref, kf_ref, vf_ref, gf_ref, of_ref), (qb_ref, kb_ref, vb_ref, gb_ref, ob_ref))

    @pl.when(i == 0)
    def _():
        st_scr[...] = s0_ref[:, 0]

    r = lax.broadcasted_iota(jnp.int32, (grp, grp), 0)
    c = lax.broadcasted_iota(jnp.int32, (grp, grp), 1)
    same = (r // GLA_CHUNK) == (c // GLA_CHUNK)
    for d, refs in enumerate(dirs):
        tri = jnp.where(same & ((c >= r) if d else (c <= r)), 1.0, 0.0).astype(BF16)
        for gi in range(t // grp):
            g = refs[3][0, gi * grp:(gi + 1) * grp, :]
            hi = g.astype(BF16)
            r1 = g - hi.astype(F32)
            mid = r1.astype(BF16)
            lo = (r1 - mid.astype(F32)).astype(BF16)
            b_scr[d, gi * grp:(gi + 1) * grp, :] = _dot(tri, hi) + _dot(tri, mid) + _dot(tri, lo)

    rr = lax.broadcasted_iota(jnp.int32, (GLA_CHUNK, GLA_CHUNK), 0)
    cc = lax.broadcasted_iota(jnp.int32, (GLA_CHUNK, GLA_CHUNK), 1)

    def chunk(j, _):
        for d, (q_ref, k_ref, v_ref, _, o_ref) in enumerate(dirs):
            n = (n_chunks - 1 - j) if d else j
            keep = (cc >= rr) if d else (cc <= rr)
            edge = 0 if d else GLA_CHUNK - 1
            rows = pl.ds(pl.multiple_of(n * GLA_CHUNK, GLA_CHUNK), GLA_CHUNK)
            for hh in range(GLA_HEADS):
                cols = slice(hh * GLA_HEAD_K, (hh + 1) * GLA_HEAD_K)
                bc = b_scr[d, rows, cols]
                b_all = bc[edge:edge + 1, :]
                qc = q_ref[0, rows, cols].astype(F32)
                kc = k_ref[0, rows, cols].astype(F32)
                vc = v_ref[0, rows, cols]
                qe = (qc * jnp.exp(bc)).astype(BF16)
                ke = (kc * jnp.exp(-bc)).astype(BF16)
                kd = (kc * jnp.exp(b_all - bc)).astype(BF16)
                st = st_scr[d, hh]
                a = jnp.where(keep, _dot_nt(qe, ke), 0.0).astype(BF16)
                o = _dot(a, vc) + _dot_nt(qe, st.astype(BF16))
                o_ref[0, rows, cols] = o.astype(o_ref.dtype)
                st_scr[d, hh] = st * jnp.exp(b_all) + _dot_tn(vc, kd)
        return 0

    lax.fori_loop(0, n_chunks, chunk, 0, unroll=2)

    @pl.when(i == pl.num_programs(1) - 1)
    def _():
        sout_ref[:, 0] = st_scr[...]


def _gla(q, k, v, g, s0, *, tile):
    b, l, _ = q.shape
    tile = min(tile, l)
    nt = l // tile
    fwd = pl.BlockSpec((1, tile, GLA_KEY), lambda bi, i: (bi, i, 0))
    bwd = pl.BlockSpec((1, tile, GLA_KEY), lambda bi, i: (bi, nt - 1 - i, 0))
    st_spec = pl.BlockSpec((2, 1, GLA_HEADS, GLA_HEAD_V, GLA_HEAD_K), lambda bi, i: (0, bi, 0, 0, 0))
    return pl.pallas_call(
        _gla_kernel,
        grid=(b, nt),
        in_specs=[fwd, fwd, fwd, fwd, bwd, bwd, bwd,
                  pl.BlockSpec((1, tile, GLA_KEY), lambda bi, i: (bi, nt - 1 - i, 1)),
                  st_spec],
        out_specs=[fwd, bwd, st_spec],
        out_shape=[jax.ShapeDtypeStruct((b, l, GLA_VALUE), BF16), jax.ShapeDtypeStruct((b, l, GLA_VALUE), BF16),
                   jax.ShapeDtypeStruct((2, b, GLA_HEADS, GLA_HEAD_V, GLA_HEAD_K), F32)],
        scratch_shapes=[pltpu.VMEM((2, GLA_HEADS, GLA_HEAD_V, GLA_HEAD_K), F32),
                        pltpu.VMEM((2, tile, GLA_KEY), F32)],
        compiler_params=_params(2),
        name="gla",
    )(q, k, v, g, q, k, v, g, s0)


def _mix_kernel(x_ref, sh_ref, sc_ref, gt_ref, g1_ref, om_ref, of_ref, ob_ref, wb_ref, gng_ref,
                wbm_ref, wbg_ref, wo_ref, o_ref):
    x = x_ref[0]
    h = (_rms(x, g1_ref[...]) * (1.0 + sc_ref[0]) + sh_ref[0]).astype(BF16)
    zr = _dot(h, wb_ref[:, 0:GLA_VALUE])
    og = of_ref[0].astype(F32) + ob_ref[0].astype(F32)
    ys = []
    for hh in range(GLA_HEADS):
        cols = slice(hh * GLA_HEAD_V, (hh + 1) * GLA_HEAD_V)
        r = zr[:, cols]
        ys.append((_rms(og[:, cols], gng_ref[...]) * (r * jax.nn.sigmoid(r))).astype(BF16))
    br_gla = _dot(jnp.concatenate(ys, axis=1), wbg_ref[...])
    br_mla = _dot(om_ref[0], wbm_ref[...])
    g_mla = jax.nn.sigmoid(_dot(h, wb_ref[:, GLA_VALUE:GLA_VALUE + D_MODEL]))
    g_gla = jax.nn.sigmoid(_dot(h, wb_ref[:, GLA_VALUE + D_MODEL:]))
    merged = (g_mla * br_mla + g_gla * br_gla).astype(BF16)
    o_ref[0] = x + gt_ref[0] * _dot(merged, wo_ref[...])


def _mix(x, mod3, g1, o_mla, o_f, o_b, w_b, gng, wbm, wbg, wo, *, tm):
    b, l, _ = x.shape
    tok = lambda w: pl.BlockSpec((1, tm, w), lambda bi, i: (bi, i, 0))
    modc = lambda j: pl.BlockSpec((1, 1, D_MODEL), lambda bi, i: (bi, 0, j))
    return pl.pallas_call(
        _mix_kernel,
        grid=(b, l // tm),
        in_specs=[tok(D_MODEL), modc(0), modc(1), modc(2), _const_spec((1, D_MODEL)),
                  tok(MLA_WIDTH), tok(GLA_VALUE), tok(GLA_VALUE),
                  _const_spec(w_b.shape), _const_spec(gng.shape), _const_spec(wbm.shape),
                  _const_spec(wbg.shape), _const_spec(wo.shape)],
        out_specs=tok(D_MODEL),
        out_shape=jax.ShapeDtypeStruct((b, l, D_MODEL), F32),
        compiler_params=_params(2),
        name="mix",
    )(x, mod3, mod3, mod3, g1, o_mla, o_f, o_b, w_b, gng, wbm, wbg, wo)


FF_CHUNK = 256


def _ffn_kernel(xm_ref, xp_ref, xn_ref, sh_ref, sc_ref, gt_ref, g2_ref, wv_ref, wg_ref, cw_ref, cb_ref,
                wd_ref, fg_ref, o_ref, he_scr, gate_scr, val_scr, act_scr, acc_scr):
    i = pl.program_id(1)
    tm = xm_ref.shape[1]
    te = tm + 2 * GRID_W
    n_chunks = D_FF // FF_CHUNK

    def normed(x, keep=None):
        h = _rms(x, g2_ref[...]) * (1.0 + sc_ref[0]) + sh_ref[0]
        return (h if keep is None else h * keep).astype(BF16)

    keep_prev = jnp.where(i > 0, 1.0, 0.0).astype(F32)
    keep_next = jnp.where(i < pl.num_programs(1) - 1, 1.0, 0.0).astype(F32)
    he_scr[0:GRID_W] = normed(xp_ref[0], keep_prev)
    he_scr[GRID_W:GRID_W + tm] = normed(xm_ref[0])
    he_scr[GRID_W + tm:te] = normed(xn_ref[0], keep_next)

    col = lax.broadcasted_iota(jnp.int32, (GRID_W, 1), 0)
    has_left = col > 0
    has_right = col < GRID_W - 1

    def project(c):
        cs = slice(c * FF_CHUNK, (c + 1) * FF_CHUNK)
        gate_scr[c % 2] = _dot(he_scr[...], wg_ref[:, cs])
        val_scr[c % 2] = _dot(he_scr[GRID_W:GRID_W + tm], wv_ref[:, cs])

    def activate(c):
        slot = c % 2
        for j in range(tm // GRID_W):
            for lh in range(FF_CHUNK // 128):
                lanes = slice(lh * 128, (lh + 1) * 128)
                wl = slice(c * FF_CHUNK + lh * 128, c * FF_CHUNK + (lh + 1) * 128)
                sums = [None, None, None]
                for dr in range(3):
                    blk = gate_scr[slot, (j + dr) * GRID_W:(j + dr + 1) * GRID_W, lanes]
                    for dc in range(3):
                        term = blk * cw_ref[3 * dr + dc:3 * dr + dc + 1, wl]
                        sums[dc] = term if sums[dc] is None else sums[dc] + term
                conv = (cb_ref[:, wl] + sums[1]
                        + jnp.where(has_left, pltpu.roll(sums[0], 1, 0), 0.0)
                        + jnp.where(has_right, pltpu.roll(sums[2], GRID_W - 1, 0), 0.0))
                gelu = 0.5 * conv * (1.0 + lax.erf(conv * np.float32(2.0 ** -0.5)))
                rows = slice(j * GRID_W, (j + 1) * GRID_W)
                act_scr[slot, rows, lanes] = (gelu * val_scr[slot, rows, lanes]).astype(BF16)

    def down(c):
        part = _dot(act_scr[c % 2], wd_ref[c * FF_CHUNK:(c + 1) * FF_CHUNK, :])
        if c == 0:
            acc_scr[...] = part
        else:
            acc_scr[...] += part

    project(0)
    for s in range(n_chunks):
        if s + 1 < n_chunks:
            project(s + 1)
        activate(s)
        down(s)
    x2 = xm_ref[0] + gt_ref[0] * acc_scr[...]
    o_ref[0] = _rms(x2, fg_ref[...])


def _ffn(x1, mod3, g2, w_up, cw, cb, wd, fg, *, tm):
    b, l, _ = x1.shape
    half = lambda j: pl.BlockSpec((D_MODEL, D_FF), lambda bi, i: (0, j), pipeline_mode=pl.Buffered(1))
    rpt = tm // GRID_W
    n_rows = l // GRID_W
    tok = lambda w: pl.BlockSpec((1, tm, w), lambda bi, i: (bi, i, 0))
    modc = lambda j: pl.BlockSpec((1, 1, D_MODEL), lambda bi, i: (bi, 0, j))
    return pl.pallas_call(
        _ffn_kernel,
        grid=(b, l // tm),
        in_specs=[tok(D_MODEL),
                  pl.BlockSpec((1, GRID_W, D_MODEL), lambda bi, i: (bi, jnp.maximum(i * rpt - 1, 0), 0)),
                  pl.BlockSpec((1, GRID_W, D_MODEL),
                               lambda bi, i: (bi, jnp.minimum((i + 1) * rpt, n_rows - 1), 0)),
                  modc(3), modc(4), modc(5), _const_spec((1, D_MODEL)),
                  half(0), half(1), _const_spec(cw.shape), _const_spec(cb.shape),
                  _const_spec(wd.shape), _const_spec((1, D_MODEL))],
        out_specs=tok(D_MODEL),
        out_shape=jax.ShapeDtypeStruct((b, l, D_MODEL), F32),
        scratch_shapes=[pltpu.VMEM((tm + 2 * GRID_W, D_MODEL), BF16),
                        pltpu.VMEM((2, tm + 2 * GRID_W, FF_CHUNK), F32),
                        pltpu.VMEM((2, tm, FF_CHUNK), F32),
                        pltpu.VMEM((2, tm, FF_CHUNK), BF16),
                        pltpu.VMEM((tm, D_MODEL), F32)],
        compiler_params=_params(2),
        name="ffn",
    )(x1, x1, x1, mod3, mod3, mod3, g2, w_up, w_up, cw, cb, wd, fg)


def _rope_tables(length):
    f32 = np.float32
    t = np.arange(length)
    inv_freq = (f32(ROPE_THETA) ** (-np.arange(0, ROPE_AXIS_DIM, 2, dtype=f32) / f32(ROPE_AXIS_DIM))).astype(f32)
    row = (t // GRID_W).astype(f32)[:, None] * inv_freq
    col = (t % GRID_W).astype(f32)[:, None] * inv_freq
    ang = np.concatenate([row, row, col, col], axis=1)
    cos, sin = np.cos(ang).astype(f32), np.sin(ang).astype(f32)
    z8 = np.zeros((length, ROPE_AXIS_DIM // 2), f32)
    pad = np.zeros((length, HEAD_PAD - MLA_ROPE), f32)
    tq = f32(MLA_SCALE * LOG2E) * np.concatenate([np.ones((length, MLA_NOPE), f32), cos, sin], axis=1)
    s_row, s_col = np.sin(row).astype(f32), np.sin(col).astype(f32)
    s_next = np.concatenate([-s_row, z8, -s_col, z8, pad], axis=1)
    s_prev = np.concatenate([z8, s_row, z8, s_col, pad], axis=1)
    return jnp.asarray(np.concatenate([tq, np.concatenate([cos, pad], axis=1), s_next, s_prev], axis=1))


def _ctx_tables(length):
    one = np.concatenate([np.ones((length, MLA_ROPE), np.float32),
                          np.zeros((length, HEAD_PAD - MLA_ROPE), np.float32)], 1)
    z = np.zeros((length, HEAD_PAD), np.float32)
    return jnp.asarray(np.concatenate([z, one, z, z], axis=1))


def _rot_partner(w):
    h = ROPE_AXIS_DIM // 2
    r1, r2, c1, c2 = w[..., 0:h], w[..., h:2 * h], w[..., 2 * h:3 * h], w[..., 3 * h:4 * h]
    return jnp.concatenate([-r2, r1, -c2, c1], axis=-1)


def _layout_weights(w_in, w_uq, w_ukv, w_decay, b_decay):
    offs = np.cumsum((0, MLA_Q_LORA, MLA_KV_LORA, MLA_ROPE, GLA_KEY, GLA_KEY, GLA_VALUE, GLA_VALUE,
                      2 * GATE_RANK, 2 * D_MODEL))
    part = lambda j: w_in[:, offs[j]:offs[j + 1]]
    zc = lambda n: jnp.zeros((D_MODEL, n), F32)
    w_a = jnp.concatenate([part(0), part(2), part(7), zc(HEAD_PAD - MLA_ROPE - 2 * GATE_RANK), part(1),
                           part(3), part(4), part(5)], axis=1).astype(BF16)
    w_b = jnp.concatenate([part(6), part(8)], axis=1).astype(BF16)

    uq = w_uq.reshape(MLA_Q_LORA, MLA_HEADS, MLA_NOPE + MLA_ROPE)
    rope = uq[..., MLA_NOPE:]
    wuq = jnp.concatenate([uq, _rot_partner(rope)], axis=-1).reshape(MLA_Q_LORA, MLA_HEADS * HEAD_PAD)

    ukv = w_ukv.reshape(MLA_KV_LORA, MLA_HEADS, MLA_NOPE + MLA_V_HEAD)
    k_cols = jnp.concatenate([ukv[..., :MLA_NOPE], jnp.zeros((MLA_KV_LORA, MLA_HEADS, HEAD_PAD - MLA_NOPE), F32)],
                             axis=-1).reshape(MLA_KV_LORA, MLA_HEADS * HEAD_PAD)
    v_cols = ukv[..., MLA_NOPE:].reshape(MLA_KV_LORA, MLA_WIDTH)
    eye = jnp.eye(HEAD_PAD, MLA_ROPE, dtype=F32)
    route = jnp.concatenate([jnp.zeros((HEAD_PAD, MLA_NOPE), F32), eye, eye], axis=1)
    route = jnp.tile(route, (1, MLA_HEADS))
    wk = jnp.concatenate([k_cols, route], axis=0)

    wd = jnp.zeros((HEAD_PAD, 2 * GLA_KEY), F32)
    wd = wd.at[MLA_ROPE:MLA_ROPE + GATE_RANK, :GLA_KEY].set(w_decay[0])
    wd = wd.at[MLA_ROPE + GATE_RANK:MLA_ROPE + 2 * GATE_RANK, GLA_KEY:].set(w_decay[1])
    bd = b_decay.reshape(1, 2 * GLA_KEY)
    return w_a, w_b, wuq.astype(BF16), wk.astype(BF16), v_cols.T.astype(BF16), wd.astype(BF16), bd


def kernel(x, c, ctx, c_ctx, w_ada, b_ada, norm1_g, w_in, q_norm_g, w_uq, kv_norm_g, w_ukv, gla_w_decay,
           gla_b_decay, gla_norm_g, w_br_mla, w_br_gla, w_out, norm2_g, w_up, conv_w, conv_b, w_down, final_g):
    b, l, _ = x.shape
    n_ctx = ctx.shape[1]
    assert w_ada.shape[0] == 1, "single layer: context tokens are never updated"
    row2 = lambda a: a.reshape(1, -1)

    c16 = jnp.concatenate([c, c_ctx[None], jnp.zeros((16 - b - 1, D_MODEL), F32)], axis=0)
    mod3 = _ada(c16, w_ada[0], b_ada[0]).reshape(16, 1, 6 * D_MODEL)

    w_a, w_b, wuq, wk, wvt, wd, bd = _layout_weights(w_in[0], w_uq[0], w_ukv[0], gla_w_decay[0], gla_b_decay[0])
    g1 = row2(norm1_g[0])
    proj = functools.partial(_inproj, g1=g1, w_a=w_a, qg=row2(q_norm_g[0]), wuq=wuq, kvg=row2(kv_norm_g[0]),
                             wk=wk, wvt=wvt, wd=wd, bd=bd, tm=TOKEN_TILE)
    q, k, v, gq, gk, gv, g = proj(x, mod3, None, tab=_rope_tables(l), with_q=True)
    k_c, v_c, gk_c, gv_c, g_c = proj(ctx, mod3, b, tab=_ctx_tables(n_ctx), with_q=False)

    o_mla = _attention(q, k, k_c, v, v_c, tq=1024)

    s0 = jnp.zeros((2, b, GLA_HEADS, GLA_HEAD_V, GLA_HEAD_K), F32)
    zq = jnp.zeros((b, n_ctx, GLA_KEY), BF16)
    _, _, s_ctx = _gla(zq, gk_c, gv_c, g_c, s0, tile=256)
    o_f, o_b, _ = _gla(gq, gk, gv, g, s_ctx, tile=512)

    x1 = _mix(x, mod3, g1, o_mla, o_f, o_b, w_b, row2(gla_norm_g[0]), w_br_mla[0].astype(BF16),
              w_br_gla[0].astype(BF16), w_out[0].astype(BF16), tm=TOKEN_TILE)

    return _ffn(x1, mod3, row2(norm2_g[0]), w_up[0].astype(BF16), conv_w[0].reshape(9, D_FF),
                row2(conv_b[0]), w_down[0].astype(BF16), row2(final_g), tm=FFN_TILE)
```

```python
import functools

import numpy as np
import jax
import jax.numpy as jnp
from jax import lax
from jax.experimental import pallas as pl
from jax.experimental.pallas import tpu as pltpu

F32 = jnp.float32
BF16 = jnp.bfloat16

D_MODEL = 1024
GRID_W = 64
EPS = 1e-6
MLA_HEADS = 8
MLA_NOPE = 64
MLA_ROPE = 32
MLA_V_HEAD = 64
MLA_Q_LORA = 384
MLA_KV_LORA = 256
MLA_WIDTH = MLA_HEADS * MLA_V_HEAD
MLA_SCALE = (MLA_NOPE + MLA_ROPE) ** -0.5
ROPE_AXIS_DIM = MLA_ROPE // 2
ROPE_THETA = 10000.0
GLA_HEADS = 4
GLA_HEAD_K = 128
GLA_HEAD_V = 128
GLA_KEY = GLA_HEADS * GLA_HEAD_K
GLA_VALUE = GLA_HEADS * GLA_HEAD_V
GLA_QSCALE = GLA_HEAD_K ** -0.5
GATE_RANK = 16
GATE_NORM = 16.0
GLA_CHUNK = 64
D_FF = 2816
LOG2E = 1.4426950408889634
TOKEN_TILE = 1024
FFN_TILE = 512
ATT_KEY_TILE = 512
ONES_ROWS = 16
HEAD_PAD = 128

VMEM_LIMIT = 56 * 1024 * 1024

A_Q = (0, 384)
A_MISC = (384, 512)
A_KV = (512, 768)
A_GQ = (768, 1280)
A_GK = (1280, 1792)
A_GV = (1792, 2304)
A_COLS = 2304


def _dot(a, b):
    return jnp.dot(a, b, preferred_element_type=F32)


def _dot_nt(a, b):
    return lax.dot_general(a, b, (((1,), (1,)), ((), ())), preferred_element_type=F32)


def _dot_tn(a, b):
    return lax.dot_general(a, b, (((0,), (0,)), ((), ())), preferred_element_type=F32)


def _rms(x, g):
    return x * lax.rsqrt(jnp.mean(x * x, axis=-1, keepdims=True) + EPS) * g


def _const_spec(shape):
    nd = len(shape)
    return pl.BlockSpec(shape, lambda *_: (0,) * nd, pipeline_mode=pl.Buffered(1))


def _params(n_grid):
    return pltpu.CompilerParams(dimension_semantics=("arbitrary",) * n_grid, vmem_limit_bytes=VMEM_LIMIT)


def _ada_kernel(c_ref, w_ref, b_ref, o_ref):
    c = c_ref[...]
    s = (c * jax.nn.sigmoid(c)).astype(BF16)
    o_ref[...] = _dot(s, w_ref[...].astype(BF16)) + b_ref[...]


def _ada(c16, w_ada, b_ada):
    n = w_ada.shape[1]
    bn = 512
    return pl.pallas_call(
        _ada_kernel,
        grid=(n // bn,),
        in_specs=[_const_spec((16, D_MODEL)),
                  pl.BlockSpec((D_MODEL, bn), lambda j: (0, j)),
                  pl.BlockSpec((1, bn), lambda j: (0, j))],
        out_specs=pl.BlockSpec((16, bn), lambda j: (0, j)),
        out_shape=jax.ShapeDtypeStruct((16, n), F32),
        compiler_params=_params(1),
        name="ada",
    )(c16, w_ada, b_ada.reshape(1, n))


def _inproj_kernel(x_ref, sh_ref, sc_ref, g1_ref, tab_ref, w_ref, qg_ref, wuq_ref, kvg_ref, wk_ref, wvt_ref,
                   wd_ref, bd_ref, *out_refs, with_q):
    if with_q:
        q_ref, k_ref, v_ref, gq_ref, gk_ref, gv_ref, g_ref = out_refs
    else:
        k_ref, v_ref, gk_ref, gv_ref, g_ref = out_refs
    x = x_ref[0]
    h = (_rms(x, g1_ref[...]) * (1.0 + sc_ref[0]) + sh_ref[0]).astype(BF16)

    if with_q:
        zqm = _dot(h, w_ref[:, A_Q[0]:A_MISC[1]])
        misc = zqm[:, A_MISC[0]:]
        qn = _rms(zqm[:, :A_Q[1]], qg_ref[...]).astype(BF16)
        qf = _dot(qn, wuq_ref[...])
        tq = tab_ref[:, 0:HEAD_PAD]
        for hh in range(MLA_HEADS):
            sl = slice(hh * HEAD_PAD, (hh + 1) * HEAD_PAD)
            q_ref[0, :, sl] = (qf[:, sl] * tq).astype(BF16)
        gq_ref[0] = (_dot(h, w_ref[:, A_GQ[0]:A_GQ[1]]) * GLA_QSCALE).astype(BF16)

    zkv = _dot(h, w_ref[:, A_KV[0]:A_KV[1]])
    kvn = _rms(zkv, kvg_ref[...]).astype(BF16)
    if not with_q:
        misc = _dot(h, w_ref[:, A_MISC[0]:A_MISC[1]])
    kr = (misc * tab_ref[:, 128:256]
          + pltpu.roll(misc, HEAD_PAD - 8, 1) * tab_ref[:, 256:384]
          + pltpu.roll(misc, 8, 1) * tab_ref[:, 384:512])
    k_ref[0] = _dot(jnp.concatenate([kvn, kr.astype(BF16)], axis=1), wk_ref[...]).astype(BF16)
    vt = _dot_nt(wvt_ref[...], kvn).astype(BF16)
    kt = v_ref.shape[3]
    for sub in range(v_ref.shape[1]):
        v_ref[0, sub] = vt[:, sub * kt:(sub + 1) * kt]

    gk_ref[0] = _dot(h, w_ref[:, A_GK[0]:A_GK[1]]).astype(BF16)
    gv_ref[0] = _dot(h, w_ref[:, A_GV[0]:A_GV[1]]).astype(BF16)

    gp = _dot(misc.astype(BF16), wd_ref[...]) + bd_ref[...]
    g_ref[0] = (jnp.minimum(gp, 0.0) - jnp.log1p(jnp.exp(-jnp.abs(gp)))) * (1.0 / GATE_NORM)


def _inproj(x, mod3, mod_row, g1, tab, w_a, qg, wuq, kvg, wk, wvt, wd, bd, *, tm, with_q):
    b, l, _ = x.shape
    tm = min(tm, l)
    nt = l // tm
    row = (lambda bi: bi) if mod_row is None else (lambda bi: mod_row)
    tok = lambda w: pl.BlockSpec((1, tm, w), lambda bi, i: (bi, i, 0))
    in_specs = [
        tok(D_MODEL),
        pl.BlockSpec((1, 1, D_MODEL), lambda bi, i: (row(bi), 0, 0)),
        pl.BlockSpec((1, 1, D_MODEL), lambda bi, i: (row(bi), 0, 1)),
        _const_spec((1, D_MODEL)),
        pl.BlockSpec((tm, 512), lambda bi, i: (i, 0)),
        _const_spec(w_a.shape), _const_spec(qg.shape), _const_spec(wuq.shape), _const_spec(kvg.shape),
        _const_spec(wk.shape), _const_spec(wvt.shape), _const_spec(wd.shape), _const_spec(bd.shape),
    ]
    sds = lambda w, dt: jax.ShapeDtypeStruct((b, l, w), dt)
    kw = MLA_HEADS * HEAD_PAD
    outs = [(kw, BF16), None, (GLA_KEY, BF16), (GLA_VALUE, BF16), (2 * GLA_KEY, F32)]
    if with_q:
        outs = [(kw, BF16)] + outs[:2] + [(GLA_KEY, BF16)] + outs[2:]
    kt = min(ATT_KEY_TILE, tm)
    vt_spec = pl.BlockSpec((1, tm // kt, MLA_WIDTH, kt), lambda bi, i: (bi, i, 0, 0))
    vt_shape = jax.ShapeDtypeStruct((b, l // kt, MLA_WIDTH, kt), BF16)
    return pl.pallas_call(
        functools.partial(_inproj_kernel, with_q=with_q),
        grid=(b, nt),
        in_specs=in_specs,
        out_specs=[vt_spec if o is None else tok(o[0]) for o in outs],
        out_shape=[vt_shape if o is None else sds(*o) for o in outs],
        compiler_params=_params(2),
        name="inproj_lat" if with_q else "inproj_ctx",
    )(x, mod3, mod3, g1, tab, w_a, qg, wuq, kvg, wk, wvt, wd, bd)


def _attn_kernel(q_ref, kl_ref, kc_ref, vl_ref, vc_ref, o_ref, sa_ref, sb_ref, sc_ref, ma_ref, mb_ref, mc_ref):
    tq = q_ref.shape[1]
    n_lat, _, tk = vl_ref.shape[1:]
    assert n_lat >= 2 and n_lat % 2 == 0

    def scores(buf, k_t):
        dst, mdst = buf
        for hh in range(2):
            cols = slice(hh * HEAD_PAD, (hh + 1) * HEAD_PAD)
            s = _dot_nt(k_t[:, cols], q_ref[0, :, cols])
            dst[hh] = s
            mdst[hh] = jnp.max(s, axis=0, keepdims=True)

    def consume(buf, vt, carry):
        src, msrc = buf
        ones = jnp.ones((ONES_ROWS, vt.shape[1]), BF16)
        out = []
        for hh in range(2):
            m, acc = carry[hh]
            m_new = jnp.maximum(m, msrc[hh])
            p = jnp.exp2(src[hh] - m_new).astype(BF16)
            v_aug = jnp.concatenate([vt[hh * MLA_V_HEAD:(hh + 1) * MLA_V_HEAD], ones], axis=0)
            out.append((m_new, acc * jnp.exp2(m - m_new) + _dot(v_aug, p)))
        return tuple(out)

    buf_a, buf_b, buf_c = (sa_ref, ma_ref), (sb_ref, mb_ref), (sc_ref, mc_ref)

    def k_tile(i):
        return kl_ref[0, pl.ds(pl.multiple_of(i * tk, tk), tk), :]

    def pair(it, carry):
        j = 2 * it
        scores(buf_b, k_tile(j + 1))
        carry = consume(buf_a, vl_ref[0, j], carry)
        scores(buf_a, k_tile(j + 2))
        return consume(buf_b, vl_ref[0, j + 1], carry)

    head0 = (jnp.full((1, tq), -jnp.inf, F32), jnp.zeros((MLA_V_HEAD + ONES_ROWS, tq), F32))
    scores(buf_a, k_tile(0))
    carry = lax.fori_loop(0, (n_lat - 2) // 2, pair, (head0, head0))
    scores(buf_b, k_tile(n_lat - 1))
    carry = consume(buf_a, vl_ref[0, n_lat - 2], carry)
    scores(buf_c, kc_ref[0])
    carry = consume(buf_b, vl_ref[0, n_lat - 1], carry)
    (_, acc0), (_, acc1) = consume(buf_c, vc_ref[0, 0], carry)
    ot = jnp.concatenate([acc[:MLA_V_HEAD] / acc[MLA_V_HEAD:MLA_V_HEAD + 1] for acc in (acc0, acc1)], axis=0)
    o_ref[0] = ot.T.astype(o_ref.dtype)


def _attention(q, k_lat, k_ctx, vt_lat, vt_ctx, *, tq):
    b, l, _ = q.shape
    c = k_ctx.shape[1]
    n_lat, _, tk = vt_lat.shape[1:]
    pairs = MLA_HEADS // 2
    return pl.pallas_call(
        _attn_kernel,
        grid=(b, pairs, l // tq),
        in_specs=[
            pl.BlockSpec((1, tq, 2 * HEAD_PAD), lambda bi, p, i: (bi, i, p)),
            pl.BlockSpec((1, l, 2 * HEAD_PAD), lambda bi, p, i: (bi, 0, p)),
            pl.BlockSpec((1, c, 2 * HEAD_PAD), lambda bi, p, i: (bi, 0, p)),
            pl.BlockSpec((1, n_lat, 2 * MLA_V_HEAD, tk), lambda bi, p, i: (bi, 0, p, 0)),
            pl.BlockSpec((1, 1, 2 * MLA_V_HEAD, c), lambda bi, p, i: (bi, 0, p, 0)),
        ],
        out_specs=pl.BlockSpec((1, tq, 2 * MLA_V_HEAD), lambda bi, p, i: (bi, i, p)),
        out_shape=jax.ShapeDtypeStruct((b, l, MLA_WIDTH), BF16),
        scratch_shapes=[pltpu.VMEM((2, tk, tq), F32), pltpu.VMEM((2, tk, tq), F32), pltpu.VMEM((2, c, tq), F32)]
        + [pltpu.VMEM((2, 1, tq), F32)] * 3,
        compiler_params=_params(3),
        name="attn",
    )(q, k_lat, k_ctx, vt_lat, vt_ctx)


def _gla_kernel(qf_ref, kf_ref, vf_ref, gf_ref, qb_ref, kb_ref, vb_ref, gb_ref, s0_ref,
                of_ref, ob_ref, sout_ref, st_scr, b_scr):
    i = pl.program_id(1)
    t = qf_ref.shape[1]
    n_chunks = t // GLA_CHUNK
    grp = min(t, 256)
    dirs = ((qf_ref, kf_ref, vf_ref, gf_ref, of_ref), (qb_ref, kb_ref, vb_ref, gb_ref, ob_ref))

    @pl.when(i == 0)
    def _():
        st_scr[...] = s0_ref[:, 0]

    r = lax.broadcasted_iota(jnp.int32, (grp, grp), 0)
    c = lax.broadcasted_iota(jnp.int32, (grp, grp), 1)
    same = (r // GLA_CHUNK) == (c // GLA_CHUNK)
    for d, refs in enumerate(dirs):
        tri = jnp.where(same & ((c >= r) if d else (c <= r)), 1.0, 0.0).astype(BF16)
        for gi in range(t // grp):
            g = refs[3][0, gi * grp:(gi + 1) * grp, :]
            hi = g.astype(BF16)
            r1 = g - hi.astype(F32)
            mid = r1.astype(BF16)
            lo = (r1 - mid.astype(F32)).astype(BF16)
            b_scr[d, gi * grp:(gi + 1) * grp, :] = _dot(tri, hi) + _dot(tri, mid) + _dot(tri, lo)

    rr = lax.broadcasted_iota(jnp.int32, (GLA_CHUNK, GLA_CHUNK), 0)
    cc = lax.broadcasted_iota(jnp.int32, (GLA_CHUNK, GLA_CHUNK), 1)

    def chunk(j, _):
        for d, (q_ref, k_ref, v_ref, _, o_ref) in enumerate(dirs):
            n = (n_chunks - 1 - j) if d else j
            keep = (cc >= rr) if d else (cc <= rr)
            edge = 0 if d else GLA_CHUNK - 1
            rows = pl.ds(pl.multiple_of(n * GLA_CHUNK, GLA_CHUNK), GLA_CHUNK)
            for hh in range(GLA_HEADS):
                cols = slice(hh * GLA_HEAD_K, (hh + 1) * GLA_HEAD_K)
                bc = b_scr[d, rows, cols]
                b_all = bc[edge:edge + 1, :]
                qc = q_ref[0, rows, cols].astype(F32)
                kc = k_ref[0, rows, cols].astype(F32)
                vc = v_ref[0, rows, cols]
                qe = (qc * jnp.exp(bc)).astype(BF16)
                ke = (kc * jnp.exp(-bc)).astype(BF16)
                kd = (kc * jnp.exp(b_all - bc)).astype(BF16)
                st = st_scr[d, hh]
                a = jnp.where(keep, _dot_nt(qe, ke), 0.0).astype(BF16)
                o = _dot(a, vc) + _dot_nt(qe, st.astype(BF16))
                o_ref[0, rows, cols] = o.astype(o_ref.dtype)
                st_scr[d, hh] = st * jnp.exp(b_all) + _dot_tn(vc, kd)
        return 0

    lax.fori_loop(0, n_chunks, chunk, 0, unroll=2)

    @pl.when(i == pl.num_programs(1) - 1)
    def _():
        sout_ref[:, 0] = st_scr[...]


def _gla(q, k, v, g, s0, *, tile):
    b, l, _ = q.shape
    tile = min(tile, l)
    nt = l // tile
    fwd = pl.BlockSpec((1, tile, GLA_KEY), lambda bi, i: (bi, i, 0))
    bwd = pl.BlockSpec((1, tile, GLA_KEY), lambda bi, i: (bi, nt - 1 - i, 0))
    st_spec = pl.BlockSpec((2, 1, GLA_HEADS, GLA_HEAD_V, GLA_HEAD_K), lambda bi, i: (0, bi, 0, 0, 0))
    return pl.pallas_call(
        _gla_kernel,
        grid=(b, nt),
        in_specs=[fwd, fwd, fwd, fwd, bwd, bwd, bwd,
                  pl.BlockSpec((1, tile, GLA_KEY), lambda bi, i: (bi, nt - 1 - i, 1)),
                  st_spec],
        out_specs=[fwd, bwd, st_spec],
        out_shape=[jax.ShapeDtypeStruct((b, l, GLA_VALUE), BF16), jax.ShapeDtypeStruct((b, l, GLA_VALUE), BF16),
                   jax.ShapeDtypeStruct((2, b, GLA_HEADS, GLA_HEAD_V, GLA_HEAD_K), F32)],
        scratch_shapes=[pltpu.VMEM((2, GLA_HEADS, GLA_HEAD_V, GLA_HEAD_K), F32),
                        pltpu.VMEM((2, tile, GLA_KEY), F32)],
        compiler_params=_params(2),
        name="gla",
    )(q, k, v, g, q, k, v, g, s0)


def _mix_kernel(x_ref, sh_ref, sc_ref, gt_ref, g1_ref, om_ref, of_ref, ob_ref, wb_ref, gng_ref,
                wbm_ref, wbg_ref, wo_ref, o_ref):
    x = x_ref[0]
    h = (_rms(x, g1_ref[...]) * (1.0 + sc_ref[0]) + sh_ref[0]).astype(BF16)
    zr = _dot(h, wb_ref[:, 0:GLA_VALUE])
    og = of_ref[0].astype(F32) + ob_ref[0].astype(F32)
    ys = []
    for hh in range(GLA_HEADS):
        cols = slice(hh * GLA_HEAD_V, (hh + 1) * GLA_HEAD_V)
        r = zr[:, cols]
        ys.append((_rms(og[:, cols], gng_ref[...]) * (r * jax.nn.sigmoid(r))).astype(BF16))
    br_gla = _dot(jnp.concatenate(ys, axis=1), wbg_ref[...])
    br_mla = _dot(om_ref[0], wbm_ref[...])
    g_mla = jax.nn.sigmoid(_dot(h, wb_ref[:, GLA_VALUE:GLA_VALUE + D_MODEL]))
    g_gla = jax.nn.sigmoid(_dot(h, wb_ref[:, GLA_VALUE + D_MODEL:]))
    merged = (g_mla * br_mla + g_gla * br_gla).astype(BF16)
    o_ref[0] = x + gt_ref[0] * _dot(merged, wo_ref[...])


def _mix(x, mod3, g1, o_mla, o_f, o_b, w_b, gng, wbm, wbg, wo, *, tm):
    b, l, _ = x.shape
    tok = lambda w: pl.BlockSpec((1, tm, w), lambda bi, i: (bi, i, 0))
    modc = lambda j: pl.BlockSpec((1, 1, D_MODEL), lambda bi, i: (bi, 0, j))
    return pl.pallas_call(
        _mix_kernel,
        grid=(b, l // tm),
        in_specs=[tok(D_MODEL), modc(0), modc(1), modc(2), _const_spec((1, D_MODEL)),
                  tok(MLA_WIDTH), tok(GLA_VALUE), tok(GLA_VALUE),
                  _const_spec(w_b.shape), _const_spec(gng.shape), _const_spec(wbm.shape),
                  _const_spec(wbg.shape), _const_spec(wo.shape)],
        out_specs=tok(D_MODEL),
        out_shape=jax.ShapeDtypeStruct((b, l, D_MODEL), F32),
        compiler_params=_params(2),
        name="mix",
    )(x, mod3, mod3, mod3, g1, o_mla, o_f, o_b, w_b, gng, wbm, wbg, wo)


FF_CHUNK = 256
FF_DOWN_GROUP = 6


def _ffn_kernel(xm_ref, xp_ref, xn_ref, sh_ref, sc_ref, gt_ref, g2_ref, wv_ref, wg_ref, cw_ref, cb_ref,
                wd_ref, fg_ref, o_ref, he_scr, gate_scr, val_scr, act_scr, acc_scr):
    i = pl.program_id(1)
    tm = xm_ref.shape[1]
    te = tm + 2 * GRID_W
    n_chunks = D_FF // FF_CHUNK

    def normed(x, keep=None):
        h = _rms(x, g2_ref[...]) * (1.0 + sc_ref[0]) + sh_ref[0]
        return (h if keep is None else h * keep).astype(BF16)

    keep_prev = jnp.where(i > 0, 1.0, 0.0).astype(F32)
    keep_next = jnp.where(i < pl.num_programs(1) - 1, 1.0, 0.0).astype(F32)
    he_scr[0:GRID_W] = normed(xp_ref[0], keep_prev)
    he_scr[GRID_W:GRID_W + tm] = normed(xm_ref[0])
    he_scr[GRID_W + tm:te] = normed(xn_ref[0], keep_next)

    col = lax.broadcasted_iota(jnp.int32, (GRID_W, 1), 0)
    has_left = col > 0
    has_right = col < GRID_W - 1

    def project(c):
        cs = slice(c * FF_CHUNK, (c + 1) * FF_CHUNK)
        gate_scr[c % 2] = _dot(he_scr[...], wg_ref[:, cs])
        val_scr[c % 2] = _dot(he_scr[GRID_W:GRID_W + tm], wv_ref[:, cs])

    def activate(c):
        slot = c % 2
        for j in range(tm // GRID_W):
            for lh in range(FF_CHUNK // 128):
                lanes = slice(lh * 128, (lh + 1) * 128)
                wl = slice(c * FF_CHUNK + lh * 128, c * FF_CHUNK + (lh + 1) * 128)
                sums = [None, None, None]
                for dr in range(3):
                    blk = gate_scr[slot, (j + dr) * GRID_W:(j + dr + 1) * GRID_W, lanes]
                    for dc in range(3):
                        term = blk * cw_ref[3 * dr + dc:3 * dr + dc + 1, wl]
                        sums[dc] = term if sums[dc] is None else sums[dc] + term
                conv = (cb_ref[:, wl] + sums[1]
                        + jnp.where(has_left, pltpu.roll(sums[0], 1, 0), 0.0)
                        + jnp.where(has_right, pltpu.roll(sums[2], GRID_W - 1, 0), 0.0))
                gelu = 0.5 * conv * (1.0 + lax.erf(conv * np.float32(2.0 ** -0.5)))
                rows = slice(j * GRID_W, (j + 1) * GRID_W)
                grp, pos = divmod(c, FF_DOWN_GROUP)
                cols = slice(pos * FF_CHUNK + lh * 128, pos * FF_CHUNK + (lh + 1) * 128)
                act_scr[grp % 2, rows, cols] = (gelu * val_scr[slot, rows, lanes]).astype(BF16)

    def down(grp, n):
        r0 = grp * FF_DOWN_GROUP * FF_CHUNK
        part = _dot(act_scr[grp % 2, :, 0:n * FF_CHUNK], wd_ref[r0:r0 + n * FF_CHUNK, :])
        if grp == 0:
            acc_scr[...] = part
        else:
            acc_scr[...] += part

    project(0)
    for s in range(n_chunks):
        if s + 1 < n_chunks:
            project(s + 1)
        activate(s)
        if (s + 1) % FF_DOWN_GROUP == 0 or s == n_chunks - 1:
            down(s // FF_DOWN_GROUP, s % FF_DOWN_GROUP + 1)
    x2 = xm_ref[0] + gt_ref[0] * acc_scr[...]
    o_ref[0] = _rms(x2, fg_ref[...])


def _ffn(x1, mod3, g2, w_up, cw, cb, wd, fg, *, tm):
    b, l, _ = x1.shape
    half = lambda j: pl.BlockSpec((D_MODEL, D_FF), lambda bi, i: (0, j), pipeline_mode=pl.Buffered(1))
    rpt = tm // GRID_W
    n_rows = l // GRID_W
    tok = lambda w: pl.BlockSpec((1, tm, w), lambda bi, i: (bi, i, 0))
    modc = lambda j: pl.BlockSpec((1, 1, D_MODEL), lambda bi, i: (bi, 0, j))
    return pl.pallas_call(
        _ffn_kernel,
        grid=(b, l // tm),
        in_specs=[tok(D_MODEL),
                  pl.BlockSpec((1, GRID_W, D_MODEL), lambda bi, i: (bi, jnp.maximum(i * rpt - 1, 0), 0)),
                  pl.BlockSpec((1, GRID_W, D_MODEL),
                               lambda bi, i: (bi, jnp.minimum((i + 1) * rpt, n_rows - 1), 0)),
                  modc(3), modc(4), modc(5), _const_spec((1, D_MODEL)),
                  half(0), half(1), _const_spec(cw.shape), _const_spec(cb.shape),
                  _const_spec(wd.shape), _const_spec((1, D_MODEL))],
        out_specs=tok(D_MODEL),
        out_shape=jax.ShapeDtypeStruct((b, l, D_MODEL), F32),
        scratch_shapes=[pltpu.VMEM((tm + 2 * GRID_W, D_MODEL), BF16),
                        pltpu.VMEM((2, tm + 2 * GRID_W, FF_CHUNK), F32),
                        pltpu.VMEM((2, tm, FF_CHUNK), F32),
                        pltpu.VMEM((2, tm, FF_DOWN_GROUP * FF_CHUNK), BF16),
                        pltpu.VMEM((tm, D_MODEL), F32)],
        compiler_params=_params(2),
        name="ffn",
    )(x1, x1, x1, mod3, mod3, mod3, g2, w_up, w_up, cw, cb, wd, fg)


def _rope_tables(length):
    f32 = np.float32
    t = np.arange(length)
    inv_freq = (f32(ROPE_THETA) ** (-np.arange(0, ROPE_AXIS_DIM, 2, dtype=f32) / f32(ROPE_AXIS_DIM))).astype(f32)
    row = (t // GRID_W).astype(f32)[:, None] * inv_freq
    col = (t % GRID_W).astype(f32)[:, None] * inv_freq
    ang = np.concatenate([row, row, col, col], axis=1)
    cos, sin = np.cos(ang).astype(f32), np.sin(ang).astype(f32)
    z8 = np.zeros((length, ROPE_AXIS_DIM // 2), f32)
    pad = np.zeros((length, HEAD_PAD - MLA_ROPE), f32)
    tq = f32(MLA_SCALE * LOG2E) * np.concatenate([np.ones((length, MLA_NOPE), f32), cos, sin], axis=1)
    s_row, s_col = np.sin(row).astype(f32), np.sin(col).astype(f32)
    s_next = np.concatenate([-s_row, z8, -s_col, z8, pad], axis=1)
    s_prev = np.concatenate([z8, s_row, z8, s_col, pad], axis=1)
    return jnp.asarray(np.concatenate([tq, np.concatenate([cos, pad], axis=1), s_next, s_prev], axis=1))


def _ctx_tables(length):
    one = np.concatenate([np.ones((length, MLA_ROPE), np.float32),
                          np.zeros((length, HEAD_PAD - MLA_ROPE), np.float32)], 1)
    z = np.zeros((length, HEAD_PAD), np.float32)
    return jnp.asarray(np.concatenate([z, one, z, z], axis=1))


def _rot_partner(w):
    h = ROPE_AXIS_DIM // 2
    r1, r2, c1, c2 = w[..., 0:h], w[..., h:2 * h], w[..., 2 * h:3 * h], w[..., 3 * h:4 * h]
    return jnp.concatenate([-r2, r1, -c2, c1], axis=-1)


def _layout_weights(w_in, w_uq, w_ukv, w_decay, b_decay):
    offs = np.cumsum((0, MLA_Q_LORA, MLA_KV_LORA, MLA_ROPE, GLA_KEY, GLA_KEY, GLA_VALUE, GLA_VALUE,
                      2 * GATE_RANK, 2 * D_MODEL))
    part = lambda j: w_in[:, offs[j]:offs[j + 1]]
    zc = lambda n: jnp.zeros((D_MODEL, n), F32)
    w_a = jnp.concatenate([part(0), part(2), part(7), zc(HEAD_PAD - MLA_ROPE - 2 * GATE_RANK), part(1),
                           part(3), part(4), part(5)], axis=1).astype(BF16)
    w_b = jnp.concatenate([part(6), part(8)], axis=1).astype(BF16)

    uq = w_uq.reshape(MLA_Q_LORA, MLA_HEADS, MLA_NOPE + MLA_ROPE)
    rope = uq[..., MLA_NOPE:]
    wuq = jnp.concatenate([uq, _rot_partner(rope)], axis=-1).reshape(MLA_Q_LORA, MLA_HEADS * HEAD_PAD)

    ukv = w_ukv.reshape(MLA_KV_LORA, MLA_HEADS, MLA_NOPE + MLA_V_HEAD)
    k_cols = jnp.concatenate([ukv[..., :MLA_NOPE], jnp.zeros((MLA_KV_LORA, MLA_HEADS, HEAD_PAD - MLA_NOPE), F32)],
                             axis=-1).reshape(MLA_KV_LORA, MLA_HEADS * HEAD_PAD)
    v_cols = ukv[..., MLA_NOPE:].reshape(MLA_KV_LORA, MLA_WIDTH)
    eye = jnp.eye(HEAD_PAD, MLA_ROPE, dtype=F32)
    route = jnp.concatenate([jnp.zeros((HEAD_PAD, MLA_NOPE), F32), eye, eye], axis=1)
    route = jnp.tile(route, (1, MLA_HEADS))
    wk = jnp.concatenate([k_cols, route], axis=0)

    wd = jnp.zeros((HEAD_PAD, 2 * GLA_KEY), F32)
    wd = wd.at[MLA_ROPE:MLA_ROPE + GATE_RANK, :GLA_KEY].set(w_decay[0])
    wd = wd.at[MLA_ROPE + GATE_RANK:MLA_ROPE + 2 * GATE_RANK, GLA_KEY:].set(w_decay[1])
    bd = b_decay.reshape(1, 2 * GLA_KEY)
    return w_a, w_b, wuq.astype(BF16), wk.astype(BF16), v_cols.T.astype(BF16), wd.astype(BF16), bd


def kernel(x, c, ctx, c_ctx, w_ada, b_ada, norm1_g, w_in, q_norm_g, w_uq, kv_norm_g, w_ukv, gla_w_decay,
           gla_b_decay, gla_norm_g, w_br_mla, w_br_gla, w_out, norm2_g, w_up, conv_w, conv_b, w_down, final_g):
    b, l, _ = x.shape
    n_ctx = ctx.shape[1]
    assert w_ada.shape[0] == 1, "single layer: context tokens are never updated"
    row2 = lambda a: a.reshape(1, -1)

    c16 = jnp.concatenate([c, c_ctx[None], jnp.zeros((16 - b - 1, D_MODEL), F32)], axis=0)
    mod3 = _ada(c16, w_ada[0], b_ada[0]).reshape(16, 1, 6 * D_MODEL)

    w_a, w_b, wuq, wk, wvt, wd, bd = _layout_weights(w_in[0], w_uq[0], w_ukv[0], gla_w_decay[0], gla_b_decay[0])
    g1 = row2(norm1_g[0])
    proj = functools.partial(_inproj, g1=g1, w_a=w_a, qg=row2(q_norm_g[0]), wuq=wuq, kvg=row2(kv_norm_g[0]),
                             wk=wk, wvt=wvt, wd=wd, bd=bd, tm=TOKEN_TILE)
    q, k, v, gq, gk, gv, g = proj(x, mod3, None, tab=_rope_tables(l), with_q=True)
    k_c, v_c, gk_c, gv_c, g_c = proj(ctx, mod3, b, tab=_ctx_tables(n_ctx), with_q=False)

    o_mla = _attention(q, k, k_c, v, v_c, tq=1024)

    s0 = jnp.zeros((2, b, GLA_HEADS, GLA_HEAD_V, GLA_HEAD_K), F32)
    zq = jnp.zeros((b, n_ctx, GLA_KEY), BF16)
    _, _, s_ctx = _gla(zq, gk_c, gv_c, g_c, s0, tile=256)
    o_f, o_b, _ = _gla(gq, gk, gv, g, s_ctx, tile=512)

    x1 = _mix(x, mod3, g1, o_mla, o_f, o_b, w_b, row2(gla_norm_g[0]), w_br_mla[0].astype(BF16),
              w_br_gla[0].astype(BF16), w_out[0].astype(BF16), tm=TOKEN_TILE)

    return _ffn(x1, mod3, row2(norm2_g[0]), w_up[0].astype(BF16), conv_w[0].reshape(9, D_FF),
                row2(conv_b[0]), w_down[0].astype(BF16), row2(final_g), tm=FFN_TILE)
```

```python
import functools

import numpy as np
import jax
import jax.numpy as jnp
from jax import lax
from jax.experimental import pallas as pl
from jax.experimental.pallas import tpu as pltpu

F32 = jnp.float32
BF16 = jnp.bfloat16

D_MODEL = 1024
GRID_W = 64
EPS = 1e-6
MLA_HEADS = 8
MLA_NOPE = 64
MLA_ROPE = 32
MLA_V_HEAD = 64
MLA_Q_LORA = 384
MLA_KV_LORA = 256
MLA_WIDTH = MLA_HEADS * MLA_V_HEAD
MLA_SCALE = (MLA_NOPE + MLA_ROPE) ** -0.5
ROPE_AXIS_DIM = MLA_ROPE // 2
ROPE_THETA = 10000.0
GLA_HEADS = 4
GLA_HEAD_K = 128
GLA_HEAD_V = 128
GLA_KEY = GLA_HEADS * GLA_HEAD_K
GLA_VALUE = GLA_HEADS * GLA_HEAD_V
GLA_QSCALE = GLA_HEAD_K ** -0.5
GATE_RANK = 16
GATE_NORM = 16.0
GLA_CHUNK = 64
D_FF = 2816
LOG2E = 1.4426950408889634
TOKEN_TILE = 1024
FFN_TILE = 512
ATT_KEY_TILE = 512
ONES_ROWS = 16
HEAD_PAD = 128

VMEM_LIMIT = 56 * 1024 * 1024

A_Q = (0, 384)
A_MISC = (384, 512)
A_KV = (512, 768)
A_GQ = (768, 1280)
A_GK = (1280, 1792)
A_GV = (1792, 2304)
A_COLS = 2304


def _dot(a, b):
    return jnp.dot(a, b, preferred_element_type=F32)


def _dot_nt(a, b):
    return lax.dot_general(a, b, (((1,), (1,)), ((), ())), preferred_element_type=F32)


def _dot_tn(a, b):
    return lax.dot_general(a, b, (((0,), (0,)), ((), ())), preferred_element_type=F32)


def _rms(x, g):
    return x * lax.rsqrt(jnp.mean(x * x, axis=-1, keepdims=True) + EPS) * g


def _const_spec(shape):
    nd = len(shape)
    return pl.BlockSpec(shape, lambda *_: (0,) * nd, pipeline_mode=pl.Buffered(1))


def _params(n_grid):
    return pltpu.CompilerParams(dimension_semantics=("arbitrary",) * n_grid, vmem_limit_bytes=VMEM_LIMIT)


def _ada_kernel(c_ref, w_ref, b_ref, o_ref):
    c = c_ref[...]
    s = (c * jax.nn.sigmoid(c)).astype(BF16)
    o_ref[...] = _dot(s, w_ref[...].astype(BF16)) + b_ref[...]


def _ada(c16, w_ada, b_ada):
    n = w_ada.shape[1]
    bn = 512
    return pl.pallas_call(
        _ada_kernel,
        grid=(n // bn,),
        in_specs=[_const_spec((16, D_MODEL)),
                  pl.BlockSpec((D_MODEL, bn), lambda j: (0, j)),
                  pl.BlockSpec((1, bn), lambda j: (0, j))],
        out_specs=pl.BlockSpec((16, bn), lambda j: (0, j)),
        out_shape=jax.ShapeDtypeStruct((16, n), F32),
        compiler_params=_params(1),
        name="ada",
    )(c16, w_ada, b_ada.reshape(1, n))


def _inproj_kernel(x_ref, sh_ref, sc_ref, g1_ref, tab_ref, w_ref, qg_ref, wuq_ref, kvg_ref, wk_ref, wvt_ref,
                   wd_ref, bd_ref, *out_refs, with_q):
    if with_q:
        q_ref, k_ref, v_ref, gq_ref, gk_ref, gv_ref, g_ref = out_refs
    else:
        k_ref, v_ref, gk_ref, gv_ref, g_ref = out_refs
    x = x_ref[0]
    h = (_rms(x, g1_ref[...]) * (1.0 + sc_ref[0]) + sh_ref[0]).astype(BF16)

    if with_q:
        zqm = _dot(h, w_ref[:, A_Q[0]:A_MISC[1]])
        misc = zqm[:, A_MISC[0]:]
        qn = _rms(zqm[:, :A_Q[1]], qg_ref[...]).astype(BF16)
        qf = _dot(qn, wuq_ref[...])
        tq = tab_ref[:, 0:HEAD_PAD]
        for hh in range(MLA_HEADS):
            sl = slice(hh * HEAD_PAD, (hh + 1) * HEAD_PAD)
            q_ref[0, :, sl] = (qf[:, sl] * tq).astype(BF16)
        gq_ref[0] = (_dot(h, w_ref[:, A_GQ[0]:A_GQ[1]]) * GLA_QSCALE).astype(BF16)

    zkv = _dot(h, w_ref[:, A_KV[0]:A_KV[1]])
    kvn = _rms(zkv, kvg_ref[...]).astype(BF16)
    if not with_q:
        misc = _dot(h, w_ref[:, A_MISC[0]:A_MISC[1]])
    kr = (misc * tab_ref[:, 128:256]
          + pltpu.roll(misc, HEAD_PAD - 8, 1) * tab_ref[:, 256:384]
          + pltpu.roll(misc, 8, 1) * tab_ref[:, 384:512])
    k_ref[0] = _dot(jnp.concatenate([kvn, kr.astype(BF16)], axis=1), wk_ref[...]).astype(BF16)
    vt = _dot_nt(wvt_ref[...], kvn).astype(BF16)
    kt = v_ref.shape[3]
    for sub in range(v_ref.shape[1]):
        v_ref[0, sub] = vt[:, sub * kt:(sub + 1) * kt]

    gk_ref[0] = _dot(h, w_ref[:, A_GK[0]:A_GK[1]]).astype(BF16)
    gv_ref[0] = _dot(h, w_ref[:, A_GV[0]:A_GV[1]]).astype(BF16)

    gp = _dot(misc.astype(BF16), wd_ref[...]) + bd_ref[...]
    g_ref[0] = (jnp.minimum(gp, 0.0) - jnp.log1p(jnp.exp(-jnp.abs(gp)))) * (1.0 / GATE_NORM)


def _inproj(x, mod3, mod_row, g1, tab, w_a, qg, wuq, kvg, wk, wvt, wd, bd, *, tm, with_q):
    b, l, _ = x.shape
    tm = min(tm, l)
    nt = l // tm
    row = (lambda bi: bi) if mod_row is None else (lambda bi: mod_row)
    tok = lambda w: pl.BlockSpec((1, tm, w), lambda bi, i: (bi, i, 0))
    in_specs = [
        tok(D_MODEL),
        pl.BlockSpec((1, 1, D_MODEL), lambda bi, i: (row(bi), 0, 0)),
        pl.BlockSpec((1, 1, D_MODEL), lambda bi, i: (row(bi), 0, 1)),
        _const_spec((1, D_MODEL)),
        pl.BlockSpec((tm, 512), lambda bi, i: (i, 0)),
        _const_spec(w_a.shape), _const_spec(qg.shape), _const_spec(wuq.shape), _const_spec(kvg.shape),
        _const_spec(wk.shape), _const_spec(wvt.shape), _const_spec(wd.shape), _const_spec(bd.shape),
    ]
    sds = lambda w, dt: jax.ShapeDtypeStruct((b, l, w), dt)
    kw = MLA_HEADS * HEAD_PAD
    outs = [(kw, BF16), None, (GLA_KEY, BF16), (GLA_VALUE, BF16), (2 * GLA_KEY, F32)]
    if with_q:
        outs = [(kw, BF16)] + outs[:2] + [(GLA_KEY, BF16)] + outs[2:]
    kt = min(ATT_KEY_TILE, tm)
    vt_spec = pl.BlockSpec((1, tm // kt, MLA_WIDTH, kt), lambda bi, i: (bi, i, 0, 0))
    vt_shape = jax.ShapeDtypeStruct((b, l // kt, MLA_WIDTH, kt), BF16)
    return pl.pallas_call(
        functools.partial(_inproj_kernel, with_q=with_q),
        grid=(b, nt),
        in_specs=in_specs,
        out_specs=[vt_spec if o is None else tok(o[0]) for o in outs],
        out_shape=[vt_shape if o is None else sds(*o) for o in outs],
        compiler_params=_params(2),
        name="inproj_lat" if with_q else "inproj_ctx",
    )(x, mod3, mod3, g1, tab, w_a, qg, wuq, kvg, wk, wvt, wd, bd)


def _attn_kernel(q_ref, kl_ref, kc_ref, vl_ref, vc_ref, o_ref, sa_ref, sb_ref, sc_ref, ma_ref, mb_ref, mc_ref):
    tq = q_ref.shape[1]
    n_lat, _, tk = vl_ref.shape[1:]
    assert n_lat >= 2 and n_lat % 2 == 0

    def scores(buf, k_t):
        dst, mdst = buf
        for hh in range(2):
            cols = slice(hh * HEAD_PAD, (hh + 1) * HEAD_PAD)
            s = _dot_nt(k_t[:, cols], q_ref[0, :, cols])
            dst[hh] = s
            mdst[hh] = jnp.max(s, axis=0, keepdims=True)

    def consume(buf, vt, carry):
        src, msrc = buf
        ones = jnp.ones((ONES_ROWS, vt.shape[1]), BF16)
        out = []
        for hh in range(2):
            m, acc = carry[hh]
            m_new = jnp.maximum(m, msrc[hh])
            p = jnp.exp2(src[hh] - m_new).astype(BF16)
            v_aug = jnp.concatenate([vt[hh * MLA_V_HEAD:(hh + 1) * MLA_V_HEAD], ones], axis=0)
            out.append((m_new, acc * jnp.exp2(m - m_new) + _dot(v_aug, p)))
        return tuple(out)

    buf_a, buf_b, buf_c = (sa_ref, ma_ref), (sb_ref, mb_ref), (sc_ref, mc_ref)

    def k_tile(i):
        return kl_ref[0, pl.ds(pl.multiple_of(i * tk, tk), tk), :]

    def pair(it, carry):
        j = 2 * it
        scores(buf_b, k_tile(j + 1))
        carry = consume(buf_a, vl_ref[0, j], carry)
        scores(buf_a, k_tile(j + 2))
        return consume(buf_b, vl_ref[0, j + 1], carry)

    head0 = (jnp.full((1, tq), -jnp.inf, F32), jnp.zeros((MLA_V_HEAD + ONES_ROWS, tq), F32))
    scores(buf_a, k_tile(0))
    carry = lax.fori_loop(0, (n_lat - 2) // 2, pair, (head0, head0))
    scores(buf_b, k_tile(n_lat - 1))
    carry = consume(buf_a, vl_ref[0, n_lat - 2], carry)
    scores(buf_c, kc_ref[0])
    carry = consume(buf_b, vl_ref[0, n_lat - 1], carry)
    (_, acc0), (_, acc1) = consume(buf_c, vc_ref[0, 0], carry)
    ot = jnp.concatenate([acc[:MLA_V_HEAD] / acc[MLA_V_HEAD:MLA_V_HEAD + 1] for acc in (acc0, acc1)], axis=0)
    o_ref[0] = ot.T.astype(o_ref.dtype)


def _attention(q, k_lat, k_ctx, vt_lat, vt_ctx, *, tq):
    b, l, _ = q.shape
    c = k_ctx.shape[1]
    n_lat, _, tk = vt_lat.shape[1:]
    pairs = MLA_HEADS // 2
    return pl.pallas_call(
        _attn_kernel,
        grid=(b, pairs, l // tq),
        in_specs=[
            pl.BlockSpec((1, tq, 2 * HEAD_PAD), lambda bi, p, i: (bi, i, p)),
            pl.BlockSpec((1, l, 2 * HEAD_PAD), lambda bi, p, i: (bi, 0, p)),
            pl.BlockSpec((1, c, 2 * HEAD_PAD), lambda bi, p, i: (bi, 0, p)),
            pl.BlockSpec((1, n_lat, 2 * MLA_V_HEAD, tk), lambda bi, p, i: (bi, 0, p, 0)),
            pl.BlockSpec((1, 1, 2 * MLA_V_HEAD, c), lambda bi, p, i: (bi, 0, p, 0)),
        ],
        out_specs=pl.BlockSpec((1, tq, 2 * MLA_V_HEAD), lambda bi, p, i: (bi, i, p)),
        out_shape=jax.ShapeDtypeStruct((b, l, MLA_WIDTH), BF16),
        scratch_shapes=[pltpu.VMEM((2, tk, tq), F32), pltpu.VMEM((2, tk, tq), F32), pltpu.VMEM((2, c, tq), F32)]
        + [pltpu.VMEM((2, 1, tq), F32)] * 3,
        compiler_params=_params(3),
        name="attn",
    )(q, k_lat, k_ctx, vt_lat, vt_ctx)


def _gla_kernel(*refs, with_out):
    if with_out:
        (qf_ref, kf_ref, vf_ref, gf_ref, qb_ref, kb_ref, vb_ref, gb_ref, s0_ref,
         of_ref, ob_ref, sout_ref, st_scr, b_scr) = refs
    else:
        kf_ref, vf_ref, gf_ref, kb_ref, vb_ref, gb_ref, s0_ref, sout_ref, st_scr, b_scr = refs
        qf_ref = qb_ref = of_ref = ob_ref = None
    i = pl.program_id(1)
    t = kf_ref.shape[1]
    n_chunks = t // GLA_CHUNK
    grp = min(t, 256)
    dirs = ((qf_ref, kf_ref, vf_ref, gf_ref, of_ref), (qb_ref, kb_ref, vb_ref, gb_ref, ob_ref))

    @pl.when(i == 0)
    def _():
        st_scr[...] = s0_ref[:, 0]

    r = lax.broadcasted_iota(jnp.int32, (grp, grp), 0)
    c = lax.broadcasted_iota(jnp.int32, (grp, grp), 1)
    same = (r // GLA_CHUNK) == (c // GLA_CHUNK)
    for d, refs_d in enumerate(dirs):
        tri = jnp.where(same & ((c >= r) if d else (c <= r)), 1.0, 0.0).astype(BF16)
        for gi in range(t // grp):
            g = refs_d[3][0, gi * grp:(gi + 1) * grp, :]
            hi = g.astype(BF16)
            r1 = g - hi.astype(F32)
            mid = r1.astype(BF16)
            lo = (r1 - mid.astype(F32)).astype(BF16)
            b_scr[d, gi * grp:(gi + 1) * grp, :] = _dot(tri, hi) + _dot(tri, mid) + _dot(tri, lo)

    rr = lax.broadcasted_iota(jnp.int32, (GLA_CHUNK, GLA_CHUNK), 0)
    cc = lax.broadcasted_iota(jnp.int32, (GLA_CHUNK, GLA_CHUNK), 1)

    def chunk(j, _):
        for d, (q_ref, k_ref, v_ref, _, o_ref) in enumerate(dirs):
            n = (n_chunks - 1 - j) if d else j
            keep = (cc >= rr) if d else (cc <= rr)
            edge = 0 if d else GLA_CHUNK - 1
            rows = pl.ds(pl.multiple_of(n * GLA_CHUNK, GLA_CHUNK), GLA_CHUNK)
            for hh in range(GLA_HEADS):
                cols = slice(hh * GLA_HEAD_K, (hh + 1) * GLA_HEAD_K)
                bc = b_scr[d, rows, cols]
                b_all = bc[edge:edge + 1, :]
                kc = k_ref[0, rows, cols].astype(F32)
                vc = v_ref[0, rows, cols]
                kd = (kc * jnp.exp(b_all - bc)).astype(BF16)
                st = st_scr[d, hh]
                if with_out:
                    qe = (q_ref[0, rows, cols].astype(F32) * jnp.exp(bc)).astype(BF16)
                    ke = (kc * jnp.exp(-bc)).astype(BF16)
                    a = jnp.where(keep, _dot_nt(qe, ke), 0.0).astype(BF16)
                    o = _dot(a, vc) + _dot_nt(qe, st.astype(BF16))
                    o_ref[0, rows, cols] = o.astype(o_ref.dtype)
                st_scr[d, hh] = st * jnp.exp(b_all) + _dot_tn(vc, kd)
        return 0

    lax.fori_loop(0, n_chunks, chunk, 0, unroll=4)

    @pl.when(i == pl.num_programs(1) - 1)
    def _():
        sout_ref[:, 0] = st_scr[...]


def _gla(q, k, v, g, s0, *, tile):
    b, l, _ = k.shape
    tile = min(tile, l)
    nt = l // tile
    with_out = q is not None
    fwd = pl.BlockSpec((1, tile, GLA_KEY), lambda bi, i: (bi, i, 0))
    bwd = pl.BlockSpec((1, tile, GLA_KEY), lambda bi, i: (bi, nt - 1 - i, 0))
    g_bwd = pl.BlockSpec((1, tile, GLA_KEY), lambda bi, i: (bi, nt - 1 - i, 1))
    st_spec = pl.BlockSpec((2, 1, GLA_HEADS, GLA_HEAD_V, GLA_HEAD_K), lambda bi, i: (0, bi, 0, 0, 0))
    st_shape = jax.ShapeDtypeStruct((2, b, GLA_HEADS, GLA_HEAD_V, GLA_HEAD_K), F32)
    o_shape = jax.ShapeDtypeStruct((b, l, GLA_VALUE), BF16)
    q_in = (q,) if with_out else ()
    return pl.pallas_call(
        functools.partial(_gla_kernel, with_out=with_out),
        grid=(b, nt),
        in_specs=[fwd] * len(q_in) + [fwd, fwd, fwd] + [bwd] * len(q_in) + [bwd, bwd, g_bwd, st_spec],
        out_specs=([fwd, bwd] if with_out else []) + [st_spec],
        out_shape=([o_shape, o_shape] if with_out else []) + [st_shape],
        scratch_shapes=[pltpu.VMEM((2, GLA_HEADS, GLA_HEAD_V, GLA_HEAD_K), F32),
                        pltpu.VMEM((2, tile, GLA_KEY), F32)],
        compiler_params=_params(2),
        name="gla" if with_out else "gla_states",
    )(*q_in, k, v, g, *q_in, k, v, g, s0)


def _mix_kernel(x_ref, sh_ref, sc_ref, gt_ref, g1_ref, om_ref, of_ref, ob_ref, wb_ref, gng_ref,
                wbm_ref, wbg_ref, wo_ref, o_ref):
    x = x_ref[0]
    h = (_rms(x, g1_ref[...]) * (1.0 + sc_ref[0]) + sh_ref[0]).astype(BF16)
    zr = _dot(h, wb_ref[:, 0:GLA_VALUE])
    og = of_ref[0].astype(F32) + ob_ref[0].astype(F32)
    ys = []
    for hh in range(GLA_HEADS):
        cols = slice(hh * GLA_HEAD_V, (hh + 1) * GLA_HEAD_V)
        r = zr[:, cols]
        ys.append((_rms(og[:, cols], gng_ref[...]) * (r * jax.nn.sigmoid(r))).astype(BF16))
    br_gla = _dot(jnp.concatenate(ys, axis=1), wbg_ref[...])
    br_mla = _dot(om_ref[0], wbm_ref[...])
    g_mla = jax.nn.sigmoid(_dot(h, wb_ref[:, GLA_VALUE:GLA_VALUE + D_MODEL]))
    g_gla = jax.nn.sigmoid(_dot(h, wb_ref[:, GLA_VALUE + D_MODEL:]))
    merged = (g_mla * br_mla + g_gla * br_gla).astype(BF16)
    o_ref[0] = x + gt_ref[0] * _dot(merged, wo_ref[...])


def _mix(x, mod3, g1, o_mla, o_f, o_b, w_b, gng, wbm, wbg, wo, *, tm):
    b, l, _ = x.shape
    tok = lambda w: pl.BlockSpec((1, tm, w), lambda bi, i: (bi, i, 0))
    modc = lambda j: pl.BlockSpec((1, 1, D_MODEL), lambda bi, i: (bi, 0, j))
    return pl.pallas_call(
        _mix_kernel,
        grid=(b, l // tm),
        in_specs=[tok(D_MODEL), modc(0), modc(1), modc(2), _const_spec((1, D_MODEL)),
                  tok(MLA_WIDTH), tok(GLA_VALUE), tok(GLA_VALUE),
                  _const_spec(w_b.shape), _const_spec(gng.shape), _const_spec(wbm.shape),
                  _const_spec(wbg.shape), _const_spec(wo.shape)],
        out_specs=tok(D_MODEL),
        out_shape=jax.ShapeDtypeStruct((b, l, D_MODEL), F32),
        compiler_params=_params(2),
        name="mix",
    )(x, mod3, mod3, mod3, g1, o_mla, o_f, o_b, w_b, gng, wbm, wbg, wo)


FF_CHUNK = 256
FF_DOWN_GROUP = 6


def _ffn_kernel(xm_ref, xp_ref, xn_ref, sh_ref, sc_ref, gt_ref, g2_ref, wv_ref, wg_ref, cw_ref, cb_ref,
                wd_ref, fg_ref, o_ref, he_scr, gate_scr, val_scr, act_scr, acc_scr):
    i = pl.program_id(1)
    tm = xm_ref.shape[1]
    te = tm + 2 * GRID_W
    n_chunks = D_FF // FF_CHUNK

    def normed(x, keep=None):
        h = _rms(x, g2_ref[...]) * (1.0 + sc_ref[0]) + sh_ref[0]
        return (h if keep is None else h * keep).astype(BF16)

    keep_prev = jnp.where(i > 0, 1.0, 0.0).astype(F32)
    keep_next = jnp.where(i < pl.num_programs(1) - 1, 1.0, 0.0).astype(F32)
    he_scr[0:GRID_W] = normed(xp_ref[0], keep_prev)
    he_scr[GRID_W:GRID_W + tm] = normed(xm_ref[0])
    he_scr[GRID_W + tm:te] = normed(xn_ref[0], keep_next)

    col = lax.broadcasted_iota(jnp.int32, (GRID_W, 1), 0)
    has_left = col > 0
    has_right = col < GRID_W - 1

    def project(c):
        cs = slice(c * FF_CHUNK, (c + 1) * FF_CHUNK)
        gate_scr[c % 2] = _dot(he_scr[...], wg_ref[:, cs])
        val_scr[c % 2] = _dot(he_scr[GRID_W:GRID_W + tm], wv_ref[:, cs])

    def activate(c):
        slot = c % 2
        for j in range(tm // GRID_W):
            for lh in range(FF_CHUNK // 128):
                lanes = slice(lh * 128, (lh + 1) * 128)
                wl = slice(c * FF_CHUNK + lh * 128, c * FF_CHUNK + (lh + 1) * 128)
                sums = [None, None, None]
                for dr in range(3):
                    blk = gate_scr[slot, (j + dr) * GRID_W:(j + dr + 1) * GRID_W, lanes]
                    for dc in range(3):
                        term = blk * cw_ref[3 * dr + dc:3 * dr + dc + 1, wl]
                        sums[dc] = term if sums[dc] is None else sums[dc] + term
                conv = (cb_ref[:, wl] + sums[1]
                        + jnp.where(has_left, pltpu.roll(sums[0], 1, 0), 0.0)
                        + jnp.where(has_right, pltpu.roll(sums[2], GRID_W - 1, 0), 0.0))
                gelu = 0.5 * conv * (1.0 + lax.erf(conv * np.float32(2.0 ** -0.5)))
                rows = slice(j * GRID_W, (j + 1) * GRID_W)
                grp, pos = divmod(c, FF_DOWN_GROUP)
                cols = slice(pos * FF_CHUNK + lh * 128, pos * FF_CHUNK + (lh + 1) * 128)
                act_scr[grp % 2, rows, cols] = (gelu * val_scr[slot, rows, lanes]).astype(BF16)

    def down(grp, n):
        r0 = grp * FF_DOWN_GROUP * FF_CHUNK
        part = _dot(act_scr[grp % 2, :, 0:n * FF_CHUNK], wd_ref[r0:r0 + n * FF_CHUNK, :])
        if grp == 0:
            acc_scr[...] = part
        else:
            acc_scr[...] += part

    project(0)
    for s in range(n_chunks):
        if s + 1 < n_chunks:
            project(s + 1)
        activate(s)
        if (s + 1) % FF_DOWN_GROUP == 0 or s == n_chunks - 1:
            down(s // FF_DOWN_GROUP, s % FF_DOWN_GROUP + 1)
    x2 = xm_ref[0] + gt_ref[0] * acc_scr[...]
    o_ref[0] = _rms(x2, fg_ref[...])


def _ffn(x1, mod3, g2, w_up, cw, cb, wd, fg, *, tm):
    b, l, _ = x1.shape
    half = lambda j: pl.BlockSpec((D_MODEL, D_FF), lambda bi, i: (0, j), pipeline_mode=pl.Buffered(1))
    rpt = tm // GRID_W
    n_rows = l // GRID_W
    tok = lambda w: pl.BlockSpec((1, tm, w), lambda bi, i: (bi, i, 0))
    modc = lambda j: pl.BlockSpec((1, 1, D_MODEL), lambda bi, i: (bi, 0, j))
    return pl.pallas_call(
        _ffn_kernel,
        grid=(b, l // tm),
        in_specs=[tok(D_MODEL),
                  pl.BlockSpec((1, GRID_W, D_MODEL), lambda bi, i: (bi, jnp.maximum(i * rpt - 1, 0), 0)),
                  pl.BlockSpec((1, GRID_W, D_MODEL),
                               lambda bi, i: (bi, jnp.minimum((i + 1) * rpt, n_rows - 1), 0)),
                  modc(3), modc(4), modc(5), _const_spec((1, D_MODEL)),
                  half(0), half(1), _const_spec(cw.shape), _const_spec(cb.shape),
                  _const_spec(wd.shape), _const_spec((1, D_MODEL))],
        out_specs=tok(D_MODEL),
        out_shape=jax.ShapeDtypeStruct((b, l, D_MODEL), F32),
        scratch_shapes=[pltpu.VMEM((tm + 2 * GRID_W, D_MODEL), BF16),
                        pltpu.VMEM((2, tm + 2 * GRID_W, FF_CHUNK), F32),
                        pltpu.VMEM((2, tm, FF_CHUNK), F32),
                        pltpu.VMEM((2, tm, FF_DOWN_GROUP * FF_CHUNK), BF16),
                        pltpu.VMEM((tm, D_MODEL), F32)],
        compiler_params=_params(2),
        name="ffn",
    )(x1, x1, x1, mod3, mod3, mod3, g2, w_up, w_up, cw, cb, wd, fg)


def _rope_tables(length):
    f32 = np.float32
    t = np.arange(length)
    inv_freq = (f32(ROPE_THETA) ** (-np.arange(0, ROPE_AXIS_DIM, 2, dtype=f32) / f32(ROPE_AXIS_DIM))).astype(f32)
    row = (t // GRID_W).astype(f32)[:, None] * inv_freq
    col = (t % GRID_W).astype(f32)[:, None] * inv_freq
    ang = np.concatenate([row, row, col, col], axis=1)
    cos, sin = np.cos(ang).astype(f32), np.sin(ang).astype(f32)
    z8 = np.zeros((length, ROPE_AXIS_DIM // 2), f32)
    pad = np.zeros((length, HEAD_PAD - MLA_ROPE), f32)
    tq = f32(MLA_SCALE * LOG2E) * np.concatenate([np.ones((length, MLA_NOPE), f32), cos, sin], axis=1)
    s_row, s_col = np.sin(row).astype(f32), np.sin(col).astype(f32)
    s_next = np.concatenate([-s_row, z8, -s_col, z8, pad], axis=1)
    s_prev = np.concatenate([z8, s_row, z8, s_col, pad], axis=1)
    return jnp.asarray(np.concatenate([tq, np.concatenate([cos, pad], axis=1), s_next, s_prev], axis=1))


def _ctx_tables(length):
    one = np.concatenate([np.ones((length, MLA_ROPE), np.float32),
                          np.zeros((length, HEAD_PAD - MLA_ROPE), np.float32)], 1)
    z = np.zeros((length, HEAD_PAD), np.float32)
    return jnp.asarray(np.concatenate([z, one, z, z], axis=1))


def _rot_partner(w):
    h = ROPE_AXIS_DIM // 2
    r1, r2, c1, c2 = w[..., 0:h], w[..., h:2 * h], w[..., 2 * h:3 * h], w[..., 3 * h:4 * h]
    return jnp.concatenate([-r2, r1, -c2, c1], axis=-1)


def _layout_weights(w_in, w_uq, w_ukv, w_decay, b_decay):
    offs = np.cumsum((0, MLA_Q_LORA, MLA_KV_LORA, MLA_ROPE, GLA_KEY, GLA_KEY, GLA_VALUE, GLA_VALUE,
                      2 * GATE_RANK, 2 * D_MODEL))
    part = lambda j: w_in[:, offs[j]:offs[j + 1]]
    zc = lambda n: jnp.zeros((D_MODEL, n), F32)
    w_a = jnp.concatenate([part(0), part(2), part(7), zc(HEAD_PAD - MLA_ROPE - 2 * GATE_RANK), part(1),
                           part(3), part(4), part(5)], axis=1).astype(BF16)
    w_b = jnp.concatenate([part(6), part(8)], axis=1).astype(BF16)

    uq = w_uq.reshape(MLA_Q_LORA, MLA_HEADS, MLA_NOPE + MLA_ROPE)
    rope = uq[..., MLA_NOPE:]
    wuq = jnp.concatenate([uq, _rot_partner(rope)], axis=-1).reshape(MLA_Q_LORA, MLA_HEADS * HEAD_PAD)

    ukv = w_ukv.reshape(MLA_KV_LORA, MLA_HEADS, MLA_NOPE + MLA_V_HEAD)
    k_cols = jnp.concatenate([ukv[..., :MLA_NOPE], jnp.zeros((MLA_KV_LORA, MLA_HEADS, HEAD_PAD - MLA_NOPE), F32)],
                             axis=-1).reshape(MLA_KV_LORA, MLA_HEADS * HEAD_PAD)
    v_cols = ukv[..., MLA_NOPE:].reshape(MLA_KV_LORA, MLA_WIDTH)
    eye = jnp.eye(HEAD_PAD, MLA_ROPE, dtype=F32)
    route = jnp.concatenate([jnp.zeros((HEAD_PAD, MLA_NOPE), F32), eye, eye], axis=1)
    route = jnp.tile(route, (1, MLA_HEADS))
    wk = jnp.concatenate([k_cols, route], axis=0)

    wd = jnp.zeros((HEAD_PAD, 2 * GLA_KEY), F32)
    wd = wd.at[MLA_ROPE:MLA_ROPE + GATE_RANK, :GLA_KEY].set(w_decay[0])
    wd = wd.at[MLA_ROPE + GATE_RANK:MLA_ROPE + 2 * GATE_RANK, GLA_KEY:].set(w_decay[1])
    bd = b_decay.reshape(1, 2 * GLA_KEY)
    return w_a, w_b, wuq.astype(BF16), wk.astype(BF16), v_cols.T.astype(BF16), wd.astype(BF16), bd


def kernel(x, c, ctx, c_ctx, w_ada, b_ada, norm1_g, w_in, q_norm_g, w_uq, kv_norm_g, w_ukv, gla_w_decay,
           gla_b_decay, gla_norm_g, w_br_mla, w_br_gla, w_out, norm2_g, w_up, conv_w, conv_b, w_down, final_g):
    b, l, _ = x.shape
    n_ctx = ctx.shape[1]
    assert w_ada.shape[0] == 1, "single layer: context tokens are never updated"
    row2 = lambda a: a.reshape(1, -1)

    c16 = jnp.concatenate([c, c_ctx[None], jnp.zeros((16 - b - 1, D_MODEL), F32)], axis=0)
    mod3 = _ada(c16, w_ada[0], b_ada[0]).reshape(16, 1, 6 * D_MODEL)

    w_a, w_b, wuq, wk, wvt, wd, bd = _layout_weights(w_in[0], w_uq[0], w_ukv[0], gla_w_decay[0], gla_b_decay[0])
    g1 = row2(norm1_g[0])
    proj = functools.partial(_inproj, g1=g1, w_a=w_a, qg=row2(q_norm_g[0]), wuq=wuq, kvg=row2(kv_norm_g[0]),
                             wk=wk, wvt=wvt, wd=wd, bd=bd, tm=TOKEN_TILE)
    q, k, v, gq, gk, gv, g = proj(x, mod3, None, tab=_rope_tables(l), with_q=True)
    k_c, v_c, gk_c, gv_c, g_c = proj(ctx, mod3, b, tab=_ctx_tables(n_ctx), with_q=False)

    o_mla = _attention(q, k, k_c, v, v_c, tq=1024)

    s0 = jnp.zeros((2, b, GLA_HEADS, GLA_HEAD_V, GLA_HEAD_K), F32)
    (s_ctx,) = _gla(None, gk_c, gv_c, g_c, s0, tile=256)
    o_f, o_b, _ = _gla(gq, gk, gv, g, s_ctx, tile=512)

    x1 = _mix(x, mod3, g1, o_mla, o_f, o_b, w_b, row2(gla_norm_g[0]), w_br_mla[0].astype(BF16),
              w_br_gla[0].astype(BF16), w_out[0].astype(BF16), tm=TOKEN_TILE)

    return _ffn(x1, mod3, row2(norm2_g[0]), w_up[0].astype(BF16), conv_w[0].reshape(9, D_FF),
                row2(conv_b[0]), w_down[0].astype(BF16), row2(final_g), tm=FFN_TILE)
```

```python
import functools

import numpy as np
import jax
import jax.numpy as jnp
from jax import lax
from jax.experimental import pallas as pl
from jax.experimental.pallas import tpu as pltpu

F32 = jnp.float32
BF16 = jnp.bfloat16

D_MODEL = 1024
GRID_W = 64
EPS = 1e-6
MLA_HEADS = 8
MLA_NOPE = 64
MLA_ROPE = 32
MLA_V_HEAD = 64
MLA_Q_LORA = 384
MLA_KV_LORA = 256
MLA_WIDTH = MLA_HEADS * MLA_V_HEAD
MLA_SCALE = (MLA_NOPE + MLA_ROPE) ** -0.5
ROPE_AXIS_DIM = MLA_ROPE // 2
ROPE_THETA = 10000.0
GLA_HEADS = 4
GLA_HEAD_K = 128
GLA_HEAD_V = 128
GLA_KEY = GLA_HEADS * GLA_HEAD_K
GLA_VALUE = GLA_HEADS * GLA_HEAD_V
GLA_QSCALE = GLA_HEAD_K ** -0.5
GATE_RANK = 16
GATE_NORM = 16.0
GLA_CHUNK = 64
D_FF = 2816
LOG2E = 1.4426950408889634
TOKEN_TILE = 1024
FFN_TILE = 512
ATT_KEY_TILE = 512
ONES_ROWS = 16
HEAD_PAD = 128

VMEM_LIMIT = 56 * 1024 * 1024

A_Q = (0, 384)
A_MISC = (384, 512)
A_KV = (512, 768)
A_GQ = (768, 1280)
A_GK = (1280, 1792)
A_GV = (1792, 2304)


def _dot(a, b):
    return jnp.dot(a, b, preferred_element_type=F32)


def _dot_nt(a, b):
    return lax.dot_general(a, b, (((1,), (1,)), ((), ())), preferred_element_type=F32)


def _dot_tn(a, b):
    return lax.dot_general(a, b, (((0,), (0,)), ((), ())), preferred_element_type=F32)


def _rms(x, g):
    return x * lax.rsqrt(jnp.mean(x * x, axis=-1, keepdims=True) + EPS) * g


def _const_spec(shape):
    nd = len(shape)
    return pl.BlockSpec(shape, lambda *_: (0,) * nd, pipeline_mode=pl.Buffered(1))


def _params(n_grid):
    return pltpu.CompilerParams(dimension_semantics=("arbitrary",) * n_grid, vmem_limit_bytes=VMEM_LIMIT)


def _ada_kernel(c_ref, w_ref, b_ref, o_ref):
    c = c_ref[...]
    s = (c * jax.nn.sigmoid(c)).astype(BF16)
    o_ref[...] = _dot(s, w_ref[...].astype(BF16)) + b_ref[...]


def _ada(c16, w_ada, b_ada):
    n = w_ada.shape[1]
    bn = 512
    return pl.pallas_call(
        _ada_kernel,
        grid=(n // bn,),
        in_specs=[_const_spec((16, D_MODEL)),
                  pl.BlockSpec((D_MODEL, bn), lambda j: (0, j)),
                  pl.BlockSpec((1, bn), lambda j: (0, j))],
        out_specs=pl.BlockSpec((16, bn), lambda j: (0, j)),
        out_shape=jax.ShapeDtypeStruct((16, n), F32),
        compiler_params=_params(1),
        name="ada",
    )(c16, w_ada, b_ada.reshape(1, n))


def _inproj_kernel(x_ref, sh_ref, sc_ref, g1_ref, tab_ref, w_ref, qg_ref, wuq_ref, kvg_ref, wk_ref, wvt_ref,
                   wd_ref, bd_ref, *out_refs, with_q):
    if with_q:
        q_ref, k_ref, v_ref, gq_ref, gk_ref, gv_ref, g_ref = out_refs
    else:
        k_ref, v_ref, gk_ref, gv_ref, g_ref = out_refs
    x = x_ref[0]
    h = (_rms(x, g1_ref[...]) * (1.0 + sc_ref[0]) + sh_ref[0]).astype(BF16)

    if with_q:
        zqm = _dot(h, w_ref[:, A_Q[0]:A_MISC[1]])
        misc = zqm[:, A_MISC[0]:]
        qn = _rms(zqm[:, :A_Q[1]], qg_ref[...]).astype(BF16)
        qf = _dot(qn, wuq_ref[...])
        tq = tab_ref[:, 0:HEAD_PAD]
        for hh in range(MLA_HEADS):
            sl = slice(hh * HEAD_PAD, (hh + 1) * HEAD_PAD)
            q_ref[0, :, sl] = (qf[:, sl] * tq).astype(BF16)
        gq_ref[0] = (_dot(h, w_ref[:, A_GQ[0]:A_GQ[1]]) * GLA_QSCALE).astype(BF16)

    zkv = _dot(h, w_ref[:, A_KV[0]:A_KV[1]])
    kvn = _rms(zkv, kvg_ref[...]).astype(BF16)
    if not with_q:
        misc = _dot(h, w_ref[:, A_MISC[0]:A_MISC[1]])
    kr = (misc * tab_ref[:, 128:256]
          + pltpu.roll(misc, HEAD_PAD - 8, 1) * tab_ref[:, 256:384]
          + pltpu.roll(misc, 8, 1) * tab_ref[:, 384:512])
    k_ref[0] = _dot(jnp.concatenate([kvn, kr.astype(BF16)], axis=1), wk_ref[...]).astype(BF16)
    vt = _dot_nt(wvt_ref[...], kvn).astype(BF16)
    kt = v_ref.shape[3]
    for sub in range(v_ref.shape[1]):
        v_ref[0, sub] = vt[:, sub * kt:(sub + 1) * kt]

    gk_ref[0] = _dot(h, w_ref[:, A_GK[0]:A_GK[1]]).astype(BF16)
    gv_ref[0] = _dot(h, w_ref[:, A_GV[0]:A_GV[1]]).astype(BF16)

    gp = _dot(misc.astype(BF16), wd_ref[...]) + bd_ref[...]
    g_ref[0] = (jnp.minimum(gp, 0.0) - jnp.log1p(jnp.exp(-jnp.abs(gp)))) * (1.0 / GATE_NORM)


def _inproj(x, mod3, mod_row, g1, tab, w_a, qg, wuq, kvg, wk, wvt, wd, bd, *, tm, with_q, key_tile=ATT_KEY_TILE):
    b, l, _ = x.shape
    tm = min(tm, l)
    nt = l // tm
    row = (lambda bi: bi) if mod_row is None else (lambda bi: mod_row)
    tok = lambda w: pl.BlockSpec((1, tm, w), lambda bi, i: (bi, i, 0))
    in_specs = [
        tok(D_MODEL),
        pl.BlockSpec((1, 1, D_MODEL), lambda bi, i: (row(bi), 0, 0)),
        pl.BlockSpec((1, 1, D_MODEL), lambda bi, i: (row(bi), 0, 1)),
        _const_spec((1, D_MODEL)),
        pl.BlockSpec((tm, 512), lambda bi, i: (i, 0)),
        _const_spec(w_a.shape), _const_spec(qg.shape), _const_spec(wuq.shape), _const_spec(kvg.shape),
        _const_spec(wk.shape), _const_spec(wvt.shape), _const_spec(wd.shape), _const_spec(bd.shape),
    ]
    sds = lambda w, dt: jax.ShapeDtypeStruct((b, l, w), dt)
    kw = MLA_HEADS * HEAD_PAD
    outs = [(kw, BF16), None, (GLA_KEY, BF16), (GLA_VALUE, BF16), (2 * GLA_KEY, F32)]
    if with_q:
        outs = [(kw, BF16)] + outs[:2] + [(GLA_KEY, BF16)] + outs[2:]
    kt = min(key_tile, tm)
    vt_spec = pl.BlockSpec((1, tm // kt, MLA_WIDTH, kt), lambda bi, i: (bi, i, 0, 0))
    vt_shape = jax.ShapeDtypeStruct((b, l // kt, MLA_WIDTH, kt), BF16)
    return pl.pallas_call(
        functools.partial(_inproj_kernel, with_q=with_q),
        grid=(b, nt),
        in_specs=in_specs,
        out_specs=[vt_spec if o is None else tok(o[0]) for o in outs],
        out_shape=[vt_shape if o is None else sds(*o) for o in outs],
        compiler_params=_params(2),
        name="inproj_lat" if with_q else "inproj_ctx",
    )(x, mod3, mod3, g1, tab, w_a, qg, wuq, kvg, wk, wvt, wd, bd)


def _attn_kernel(q_ref, kl_ref, kc_ref, vl_ref, vc_ref, o_ref, sa_ref, sb_ref, sc_ref, ma_ref, mb_ref, mc_ref):
    tq = q_ref.shape[1]
    n_lat, _, tk = vl_ref.shape[1:]
    assert n_lat >= 2 and n_lat % 2 == 0

    def scores(buf, k_t):
        dst, mdst = buf
        for hh in range(2):
            cols = slice(hh * HEAD_PAD, (hh + 1) * HEAD_PAD)
            s = _dot_nt(k_t[:, cols], q_ref[0, :, cols])
            dst[hh] = s
            mdst[hh] = jnp.max(s, axis=0, keepdims=True)

    def consume(buf, vt, carry):
        src, msrc = buf
        ones = jnp.ones((ONES_ROWS, vt.shape[1]), BF16)
        out = []
        for hh in range(2):
            m, acc = carry[hh]
            m_new = jnp.maximum(m, msrc[hh])
            p = jnp.exp2(src[hh] - m_new).astype(BF16)
            v_aug = jnp.concatenate([vt[hh * MLA_V_HEAD:(hh + 1) * MLA_V_HEAD], ones], axis=0)
            out.append((m_new, acc * jnp.exp2(m - m_new) + _dot(v_aug, p)))
        return tuple(out)

    buf_a, buf_b, buf_c = (sa_ref, ma_ref), (sb_ref, mb_ref), (sc_ref, mc_ref)

    def k_tile(i):
        return kl_ref[0, pl.ds(pl.multiple_of(i * tk, tk), tk), :]

    def pair(it, carry):
        j = 2 * it
        scores(buf_b, k_tile(j + 1))
        carry = consume(buf_a, vl_ref[0, j], carry)
        scores(buf_a, k_tile(j + 2))
        return consume(buf_b, vl_ref[0, j + 1], carry)

    head0 = (jnp.full((1, tq), -jnp.inf, F32), jnp.zeros((MLA_V_HEAD + ONES_ROWS, tq), F32))
    scores(buf_a, k_tile(0))
    carry = lax.fori_loop(0, (n_lat - 2) // 2, pair, (head0, head0))
    scores(buf_b, k_tile(n_lat - 1))
    carry = consume(buf_a, vl_ref[0, n_lat - 2], carry)
    scores(buf_c, kc_ref[0])
    carry = consume(buf_b, vl_ref[0, n_lat - 1], carry)
    (_, acc0), (_, acc1) = consume(buf_c, vc_ref[0, 0], carry)
    ot = jnp.concatenate([acc[:MLA_V_HEAD] / acc[MLA_V_HEAD:MLA_V_HEAD + 1] for acc in (acc0, acc1)], axis=0)
    o_ref[0] = ot.T.astype(o_ref.dtype)


def _attention(q, k_lat, k_ctx, vt_lat, vt_ctx, *, tq):
    b, l, _ = q.shape
    c = k_ctx.shape[1]
    n_lat, _, tk = vt_lat.shape[1:]
    pairs = MLA_HEADS // 2
    return pl.pallas_call(
        _attn_kernel,
        grid=(b, pairs, l // tq),
        in_specs=[
            pl.BlockSpec((1, tq, 2 * HEAD_PAD), lambda bi, p, i: (bi, i, p)),
            pl.BlockSpec((1, l, 2 * HEAD_PAD), lambda bi, p, i: (bi, 0, p)),
            pl.BlockSpec((1, c, 2 * HEAD_PAD), lambda bi, p, i: (bi, 0, p)),
            pl.BlockSpec((1, n_lat, 2 * MLA_V_HEAD, tk), lambda bi, p, i: (bi, 0, p, 0)),
            pl.BlockSpec((1, 1, 2 * MLA_V_HEAD, c), lambda bi, p, i: (bi, 0, p, 0)),
        ],
        out_specs=pl.BlockSpec((1, tq, 2 * MLA_V_HEAD), lambda bi, p, i: (bi, i, p)),
        out_shape=jax.ShapeDtypeStruct((b, l, MLA_WIDTH), BF16),
        scratch_shapes=[pltpu.VMEM((2, tk, tq), F32), pltpu.VMEM((2, tk, tq), F32), pltpu.VMEM((2, c, tq), F32)]
        + [pltpu.VMEM((2, 1, tq), F32)] * 3,
        compiler_params=_params(3),
        name="attn",
    )(q, k_lat, k_ctx, vt_lat, vt_ctx)


def _gla_kernel(*refs, with_out):
    if with_out:
        (qf_ref, kf_ref, vf_ref, gf_ref, qb_ref, kb_ref, vb_ref, gb_ref, s0_ref,
         of_ref, ob_ref, sout_ref, st_scr, b_scr) = refs
    else:
        kf_ref, vf_ref, gf_ref, kb_ref, vb_ref, gb_ref, s0_ref, sout_ref, st_scr, b_scr = refs
        qf_ref = qb_ref = of_ref = ob_ref = None
    i = pl.program_id(1)
    t = kf_ref.shape[1]
    n_chunks = t // GLA_CHUNK
    grp = min(t, 256)
    dirs = ((qf_ref, kf_ref, vf_ref, gf_ref, of_ref), (qb_ref, kb_ref, vb_ref, gb_ref, ob_ref))

    @pl.when(i == 0)
    def _():
        st_scr[...] = s0_ref[:, 0]

    r = lax.broadcasted_iota(jnp.int32, (grp, grp), 0)
    c = lax.broadcasted_iota(jnp.int32, (grp, grp), 1)
    same = (r // GLA_CHUNK) == (c // GLA_CHUNK)
    for d, refs_d in enumerate(dirs):
        tri = jnp.where(same & ((c >= r) if d else (c <= r)), 1.0, 0.0).astype(BF16)
        for gi in range(t // grp):
            g = refs_d[3][0, gi * grp:(gi + 1) * grp, :]
            hi = g.astype(BF16)
            r1 = g - hi.astype(F32)
            mid = r1.astype(BF16)
            lo = (r1 - mid.astype(F32)).astype(BF16)
            b_scr[d, gi * grp:(gi + 1) * grp, :] = _dot(tri, hi) + _dot(tri, mid) + _dot(tri, lo)

    rr = lax.broadcasted_iota(jnp.int32, (GLA_CHUNK, GLA_CHUNK), 0)
    cc = lax.broadcasted_iota(jnp.int32, (GLA_CHUNK, GLA_CHUNK), 1)

    def chunk(j, _):
        for d, (q_ref, k_ref, v_ref, _, o_ref) in enumerate(dirs):
            n = (n_chunks - 1 - j) if d else j
            keep = (cc >= rr) if d else (cc <= rr)
            edge = 0 if d else GLA_CHUNK - 1
            rows = pl.ds(pl.multiple_of(n * GLA_CHUNK, GLA_CHUNK), GLA_CHUNK)
            for hh in range(GLA_HEADS):
                cols = slice(hh * GLA_HEAD_K, (hh + 1) * GLA_HEAD_K)
                bc = b_scr[d, rows, cols]
                b_all = bc[edge:edge + 1, :]
                kc = k_ref[0, rows, cols].astype(F32)
                vc = v_ref[0, rows, cols]
                kd = (kc * jnp.exp(b_all - bc)).astype(BF16)
                st = st_scr[d, hh]
                if with_out:
                    qe = (q_ref[0, rows, cols].astype(F32) * jnp.exp(bc)).astype(BF16)
                    ke = (kc * jnp.exp(-bc)).astype(BF16)
                    a = jnp.where(keep, _dot_nt(qe, ke), 0.0).astype(BF16)
                    o = _dot(a, vc) + _dot_nt(qe, st.astype(BF16))
                    o_ref[0, rows, cols] = o.astype(o_ref.dtype)
                st_scr[d, hh] = st * jnp.exp(b_all) + _dot_tn(vc, kd)
        return 0

    lax.fori_loop(0, n_chunks, chunk, 0, unroll=4)

    @pl.when(i == pl.num_programs(1) - 1)
    def _():
        sout_ref[:, 0] = st_scr[...]


def _gla(q, k, v, g, s0, *, tile):
    b, l, _ = k.shape
    tile = min(tile, l)
    nt = l // tile
    with_out = q is not None
    fwd = pl.BlockSpec((1, tile, GLA_KEY), lambda bi, i: (bi, i, 0))
    bwd = pl.BlockSpec((1, tile, GLA_KEY), lambda bi, i: (bi, nt - 1 - i, 0))
    g_bwd = pl.BlockSpec((1, tile, GLA_KEY), lambda bi, i: (bi, nt - 1 - i, 1))
    st_spec = pl.BlockSpec((2, 1, GLA_HEADS, GLA_HEAD_V, GLA_HEAD_K), lambda bi, i: (0, bi, 0, 0, 0))
    st_shape = jax.ShapeDtypeStruct((2, b, GLA_HEADS, GLA_HEAD_V, GLA_HEAD_K), F32)
    o_shape = jax.ShapeDtypeStruct((b, l, GLA_VALUE), BF16)
    q_in = (q,) if with_out else ()
    return pl.pallas_call(
        functools.partial(_gla_kernel, with_out=with_out),
        grid=(b, nt),
        in_specs=[fwd] * len(q_in) + [fwd, fwd, fwd] + [bwd] * len(q_in) + [bwd, bwd, g_bwd, st_spec],
        out_specs=([fwd, bwd] if with_out else []) + [st_spec],
        out_shape=([o_shape, o_shape] if with_out else []) + [st_shape],
        scratch_shapes=[pltpu.VMEM((2, GLA_HEADS, GLA_HEAD_V, GLA_HEAD_K), F32),
                        pltpu.VMEM((2, tile, GLA_KEY), F32)],
        compiler_params=_params(2),
        name="gla" if with_out else "gla_states",
    )(*q_in, k, v, g, *q_in, k, v, g, s0)


def _mix_kernel(x_ref, sh_ref, sc_ref, gt_ref, g1_ref, om_ref, of_ref, ob_ref, wb_ref, gng_ref,
                wbm_ref, wbg_ref, wo_ref, o_ref):
    x = x_ref[0]
    h = (_rms(x, g1_ref[...]) * (1.0 + sc_ref[0]) + sh_ref[0]).astype(BF16)
    zr = _dot(h, wb_ref[:, 0:GLA_VALUE])
    og = of_ref[0].astype(F32) + ob_ref[0].astype(F32)
    ys = []
    for hh in range(GLA_HEADS):
        cols = slice(hh * GLA_HEAD_V, (hh + 1) * GLA_HEAD_V)
        r = zr[:, cols]
        ys.append((_rms(og[:, cols], gng_ref[...]) * (r * jax.nn.sigmoid(r))).astype(BF16))
    br_gla = _dot(jnp.concatenate(ys, axis=1), wbg_ref[...])
    br_mla = _dot(om_ref[0], wbm_ref[...])
    g_mla = jax.nn.sigmoid(_dot(h, wb_ref[:, GLA_VALUE:GLA_VALUE + D_MODEL]))
    g_gla = jax.nn.sigmoid(_dot(h, wb_ref[:, GLA_VALUE + D_MODEL:]))
    merged = (g_mla * br_mla + g_gla * br_gla).astype(BF16)
    o_ref[0] = x + gt_ref[0] * _dot(merged, wo_ref[...])


def _mix(x, mod3, g1, o_mla, o_f, o_b, w_b, gng, wbm, wbg, wo, *, tm):
    b, l, _ = x.shape
    tok = lambda w: pl.BlockSpec((1, tm, w), lambda bi, i: (bi, i, 0))
    modc = lambda j: pl.BlockSpec((1, 1, D_MODEL), lambda bi, i: (bi, 0, j))
    return pl.pallas_call(
        _mix_kernel,
        grid=(b, l // tm),
        in_specs=[tok(D_MODEL), modc(0), modc(1), modc(2), _const_spec((1, D_MODEL)),
                  tok(MLA_WIDTH), tok(GLA_VALUE), tok(GLA_VALUE),
                  _const_spec(w_b.shape), _const_spec(gng.shape), _const_spec(wbm.shape),
                  _const_spec(wbg.shape), _const_spec(wo.shape)],
        out_specs=tok(D_MODEL),
        out_shape=jax.ShapeDtypeStruct((b, l, D_MODEL), F32),
        compiler_params=_params(2),
        name="mix",
    )(x, mod3, mod3, mod3, g1, o_mla, o_f, o_b, w_b, gng, wbm, wbg, wo)


FF_CHUNK = 256
FF_DOWN_GROUP = 6


def _ffn_kernel(xm_ref, xp_ref, xn_ref, sh_ref, sc_ref, gt_ref, g2_ref, wv_ref, wg_ref, cw_ref, cb_ref,
                wd_ref, fg_ref, o_ref, he_scr, gate_scr, val_scr, act_scr, acc_scr):
    i = pl.program_id(1)
    tm = xm_ref.shape[1]
    te = tm + 2 * GRID_W
    n_chunks = D_FF // FF_CHUNK

    def normed(x, keep=None):
        h = _rms(x, g2_ref[...]) * (1.0 + sc_ref[0]) + sh_ref[0]
        return (h if keep is None else h * keep).astype(BF16)

    keep_prev = jnp.where(i > 0, 1.0, 0.0).astype(F32)
    keep_next = jnp.where(i < pl.num_programs(1) - 1, 1.0, 0.0).astype(F32)
    he_scr[0:GRID_W] = normed(xp_ref[0], keep_prev)
    he_scr[GRID_W:GRID_W + tm] = normed(xm_ref[0])
    he_scr[GRID_W + tm:te] = normed(xn_ref[0], keep_next)

    col = lax.broadcasted_iota(jnp.int32, (GRID_W, 1), 0)
    has_left = col > 0
    has_right = col < GRID_W - 1

    def project(c):
        cs = slice(c * FF_CHUNK, (c + 1) * FF_CHUNK)
        gate_scr[c % 2] = _dot(he_scr[...], wg_ref[:, cs])
        val_scr[c % 2] = _dot(he_scr[GRID_W:GRID_W + tm], wv_ref[:, cs])

    def activate(c):
        slot = c % 2
        for j in range(tm // GRID_W):
            for lh in range(FF_CHUNK // 128):
                lanes = slice(lh * 128, (lh + 1) * 128)
                wl = slice(c * FF_CHUNK + lh * 128, c * FF_CHUNK + (lh + 1) * 128)
                sums = [None, None, None]
                for dr in range(3):
                    blk = gate_scr[slot, (j + dr) * GRID_W:(j + dr + 1) * GRID_W, lanes]
                    for dc in range(3):
                        term = blk * cw_ref[3 * dr + dc:3 * dr + dc + 1, wl]
                        sums[dc] = term if sums[dc] is None else sums[dc] + term
                conv = (cb_ref[:, wl] + sums[1]
                        + jnp.where(has_left, pltpu.roll(sums[0], 1, 0), 0.0)
                        + jnp.where(has_right, pltpu.roll(sums[2], GRID_W - 1, 0), 0.0))
                gelu = 0.5 * conv * (1.0 + lax.erf(conv * np.float32(2.0 ** -0.5)))
                rows = slice(j * GRID_W, (j + 1) * GRID_W)
                grp, pos = divmod(c, FF_DOWN_GROUP)
                cols = slice(pos * FF_CHUNK + lh * 128, pos * FF_CHUNK + (lh + 1) * 128)
                act_scr[grp % 2, rows, cols] = (gelu * val_scr[slot, rows, lanes]).astype(BF16)

    def down(grp, n):
        r0 = grp * FF_DOWN_GROUP * FF_CHUNK
        part = _dot(act_scr[grp % 2, :, 0:n * FF_CHUNK], wd_ref[r0:r0 + n * FF_CHUNK, :])
        if grp == 0:
            acc_scr[...] = part
        else:
            acc_scr[...] += part

    project(0)
    for s in range(n_chunks):
        if s + 1 < n_chunks:
            project(s + 1)
        activate(s)
        if (s + 1) % FF_DOWN_GROUP == 0 or s == n_chunks - 1:
            down(s // FF_DOWN_GROUP, s % FF_DOWN_GROUP + 1)
    x2 = xm_ref[0] + gt_ref[0] * acc_scr[...]
    o_ref[0] = _rms(x2, fg_ref[...])


def _ffn(x1, mod3, g2, w_up, cw, cb, wd, fg, *, tm):
    b, l, _ = x1.shape
    half = lambda j: pl.BlockSpec((D_MODEL, D_FF), lambda bi, i: (0, j), pipeline_mode=pl.Buffered(1))
    rpt = tm // GRID_W
    n_rows = l // GRID_W
    tok = lambda w: pl.BlockSpec((1, tm, w), lambda bi, i: (bi, i, 0))
    modc = lambda j: pl.BlockSpec((1, 1, D_MODEL), lambda bi, i: (bi, 0, j))
    return pl.pallas_call(
        _ffn_kernel,
        grid=(b, l // tm),
        in_specs=[tok(D_MODEL),
                  pl.BlockSpec((1, GRID_W, D_MODEL), lambda bi, i: (bi, jnp.maximum(i * rpt - 1, 0), 0)),
                  pl.BlockSpec((1, GRID_W, D_MODEL),
                               lambda bi, i: (bi, jnp.minimum((i + 1) * rpt, n_rows - 1), 0)),
                  modc(3), modc(4), modc(5), _const_spec((1, D_MODEL)),
                  half(0), half(1), _const_spec(cw.shape), _const_spec(cb.shape),
                  _const_spec(wd.shape), _const_spec((1, D_MODEL))],
        out_specs=tok(D_MODEL),
        out_shape=jax.ShapeDtypeStruct((b, l, D_MODEL), F32),
        scratch_shapes=[pltpu.VMEM((tm + 2 * GRID_W, D_MODEL), BF16),
                        pltpu.VMEM((2, tm + 2 * GRID_W, FF_CHUNK), F32),
                        pltpu.VMEM((2, tm, FF_CHUNK), F32),
                        pltpu.VMEM((2, tm, FF_DOWN_GROUP * FF_CHUNK), BF16),
                        pltpu.VMEM((tm, D_MODEL), F32)],
        compiler_params=_params(2),
        name="ffn",
    )(x1, x1, x1, mod3, mod3, mod3, g2, w_up, w_up, cw, cb, wd, fg)


def _rope_tables(length):
    f32 = np.float32
    t = np.arange(length)
    inv_freq = (f32(ROPE_THETA) ** (-np.arange(0, ROPE_AXIS_DIM, 2, dtype=f32) / f32(ROPE_AXIS_DIM))).astype(f32)
    row = (t // GRID_W).astype(f32)[:, None] * inv_freq
    col = (t % GRID_W).astype(f32)[:, None] * inv_freq
    ang = np.concatenate([row, row, col, col], axis=1)
    cos, sin = np.cos(ang).astype(f32), np.sin(ang).astype(f32)
    z8 = np.zeros((length, ROPE_AXIS_DIM // 2), f32)
    pad = np.zeros((length, HEAD_PAD - MLA_ROPE), f32)
    tq = f32(MLA_SCALE * LOG2E) * np.concatenate([np.ones((length, MLA_NOPE), f32), cos, sin], axis=1)
    s_row, s_col = np.sin(row).astype(f32), np.sin(col).astype(f32)
    s_next = np.concatenate([-s_row, z8, -s_col, z8, pad], axis=1)
    s_prev = np.concatenate([z8, s_row, z8, s_col, pad], axis=1)
    return jnp.asarray(np.concatenate([tq, np.concatenate([cos, pad], axis=1), s_next, s_prev], axis=1))


def _ctx_tables(length):
    one = np.concatenate([np.ones((length, MLA_ROPE), np.float32),
                          np.zeros((length, HEAD_PAD - MLA_ROPE), np.float32)], 1)
    z = np.zeros((length, HEAD_PAD), np.float32)
    return jnp.asarray(np.concatenate([z, one, z, z], axis=1))


def _rot_partner(w):
    h = ROPE_AXIS_DIM // 2
    r1, r2, c1, c2 = w[..., 0:h], w[..., h:2 * h], w[..., 2 * h:3 * h], w[..., 3 * h:4 * h]
    return jnp.concatenate([-r2, r1, -c2, c1], axis=-1)


def _layout_weights(w_in, w_uq, w_ukv, w_decay, b_decay):
    offs = np.cumsum((0, MLA_Q_LORA, MLA_KV_LORA, MLA_ROPE, GLA_KEY, GLA_KEY, GLA_VALUE, GLA_VALUE,
                      2 * GATE_RANK, 2 * D_MODEL))
    part = lambda j: w_in[:, offs[j]:offs[j + 1]]
    zc = lambda n: jnp.zeros((D_MODEL, n), F32)
    w_a = jnp.concatenate([part(0), part(2), part(7), zc(HEAD_PAD - MLA_ROPE - 2 * GATE_RANK), part(1),
                           part(3), part(4), part(5)], axis=1).astype(BF16)
    w_b = jnp.concatenate([part(6), part(8)], axis=1).astype(BF16)

    uq = w_uq.reshape(MLA_Q_LORA, MLA_HEADS, MLA_NOPE + MLA_ROPE)
    rope = uq[..., MLA_NOPE:]
    wuq = jnp.concatenate([uq, _rot_partner(rope)], axis=-1).reshape(MLA_Q_LORA, MLA_HEADS * HEAD_PAD)

    ukv = w_ukv.reshape(MLA_KV_LORA, MLA_HEADS, MLA_NOPE + MLA_V_HEAD)
    k_cols = jnp.concatenate([ukv[..., :MLA_NOPE], jnp.zeros((MLA_KV_LORA, MLA_HEADS, HEAD_PAD - MLA_NOPE), F32)],
                             axis=-1).reshape(MLA_KV_LORA, MLA_HEADS * HEAD_PAD)
    v_cols = ukv[..., MLA_NOPE:].reshape(MLA_KV_LORA, MLA_WIDTH)
    eye = jnp.eye(HEAD_PAD, MLA_ROPE, dtype=F32)
    route = jnp.concatenate([jnp.zeros((HEAD_PAD, MLA_NOPE), F32), eye, eye], axis=1)
    route = jnp.tile(route, (1, MLA_HEADS))
    wk = jnp.concatenate([k_cols, route], axis=0)

    wd = jnp.zeros((HEAD_PAD, 2 * GLA_KEY), F32)
    wd = wd.at[MLA_ROPE:MLA_ROPE + GATE_RANK, :GLA_KEY].set(w_decay[0])
    wd = wd.at[MLA_ROPE + GATE_RANK:MLA_ROPE + 2 * GATE_RANK, GLA_KEY:].set(w_decay[1])
    bd = b_decay.reshape(1, 2 * GLA_KEY)
    return w_a, w_b, wuq.astype(BF16), wk.astype(BF16), v_cols.T.astype(BF16), wd.astype(BF16), bd


def kernel(x, c, ctx, c_ctx, w_ada, b_ada, norm1_g, w_in, q_norm_g, w_uq, kv_norm_g, w_ukv, gla_w_decay,
           gla_b_decay, gla_norm_g, w_br_mla, w_br_gla, w_out, norm2_g, w_up, conv_w, conv_b, w_down, final_g):
    b, l, _ = x.shape
    n_ctx = ctx.shape[1]
    assert w_ada.shape[0] == 1, "single layer: context tokens are never updated"
    row2 = lambda a: a.reshape(1, -1)

    c16 = jnp.concatenate([c, c_ctx[None], jnp.zeros((16 - b - 1, D_MODEL), F32)], axis=0)
    mod3 = _ada(c16, w_ada[0], b_ada[0]).reshape(16, 1, 6 * D_MODEL)

    w_a, w_b, wuq, wk, wvt, wd, bd = _layout_weights(w_in[0], w_uq[0], w_ukv[0], gla_w_decay[0], gla_b_decay[0])
    g1 = row2(norm1_g[0])
    proj = functools.partial(_inproj, g1=g1, w_a=w_a, qg=row2(q_norm_g[0]), wuq=wuq, kvg=row2(kv_norm_g[0]),
                             wk=wk, wvt=wvt, wd=wd, bd=bd, tm=TOKEN_TILE)
    q, k, v, gq, gk, gv, g = proj(x, mod3, None, tab=_rope_tables(l), with_q=True)
    ctx_out = proj(ctx.reshape(1, b * n_ctx, D_MODEL), mod3, b, tab=_ctx_tables(b * n_ctx), with_q=False,
                   key_tile=n_ctx)
    k_c, v_c, gk_c, gv_c, g_c = (a.reshape((b, a.shape[1] // b) + a.shape[2:]) for a in ctx_out)

    o_mla = _attention(q, k, k_c, v, v_c, tq=1024)

    s0 = jnp.zeros((2, b, GLA_HEADS, GLA_HEAD_V, GLA_HEAD_K), F32)
    (s_ctx,) = _gla(None, gk_c, gv_c, g_c, s0, tile=256)
    o_f, o_b, _ = _gla(gq, gk, gv, g, s_ctx, tile=512)

    x1 = _mix(x, mod3, g1, o_mla, o_f, o_b, w_b, row2(gla_norm_g[0]), w_br_mla[0].astype(BF16),
              w_br_gla[0].astype(BF16), w_out[0].astype(BF16), tm=TOKEN_TILE)

    return _ffn(x1, mod3, row2(norm2_g[0]), w_up[0].astype(BF16), conv_w[0].reshape(9, D_FF),
                row2(conv_b[0]), w_down[0].astype(BF16), row2(final_g), tm=FFN_TILE)
```

```python
import functools

import numpy as np
import jax
import jax.numpy as jnp
from jax import lax
from jax.experimental import pallas as pl
from jax.experimental.pallas import tpu as pltpu

F32 = jnp.float32
BF16 = jnp.bfloat16

D_MODEL = 1024
GRID_W = 64
EPS = 1e-6
MLA_HEADS = 8
MLA_NOPE = 64
MLA_ROPE = 32
MLA_V_HEAD = 64
MLA_Q_LORA = 384
MLA_KV_LORA = 256
MLA_WIDTH = MLA_HEADS * MLA_V_HEAD
MLA_SCALE = (MLA_NOPE + MLA_ROPE) ** -0.5
ROPE_AXIS_DIM = MLA_ROPE // 2
ROPE_THETA = 10000.0
GLA_HEADS = 4
GLA_HEAD_K = 128
GLA_HEAD_V = 128
GLA_KEY = GLA_HEADS * GLA_HEAD_K
GLA_VALUE = GLA_HEADS * GLA_HEAD_V
GLA_QSCALE = GLA_HEAD_K ** -0.5
GATE_RANK = 16
GATE_NORM = 16.0
GLA_CHUNK = 64
D_FF = 2816
LOG2E = 1.4426950408889634
TOKEN_TILE = 1024
FFN_TILE = 512
ATT_KEY_TILE = 512
ONES_ROWS = 16
HEAD_PAD = 128

VMEM_LIMIT = 56 * 1024 * 1024

A_Q = (0, 384)
A_MISC = (384, 512)
A_KV = (512, 768)
A_GQ = (768, 1280)
A_GK = (1280, 1792)
A_GV = (1792, 2304)


def _dot(a, b):
    return jnp.dot(a, b, preferred_element_type=F32)


def _dot_nt(a, b):
    return lax.dot_general(a, b, (((1,), (1,)), ((), ())), preferred_element_type=F32)


def _dot_tn(a, b):
    return lax.dot_general(a, b, (((0,), (0,)), ((), ())), preferred_element_type=F32)


def _rms(x, g):
    return x * lax.rsqrt(jnp.mean(x * x, axis=-1, keepdims=True) + EPS) * g


def _const_spec(shape):
    nd = len(shape)
    return pl.BlockSpec(shape, lambda *_: (0,) * nd, pipeline_mode=pl.Buffered(1))


def _params(n_grid):
    return pltpu.CompilerParams(dimension_semantics=("arbitrary",) * n_grid, vmem_limit_bytes=VMEM_LIMIT)


def _ada_kernel(c_ref, w_ref, b_ref, o_ref):
    c = c_ref[...]
    s = (c * jax.nn.sigmoid(c)).astype(BF16)
    o_ref[...] = _dot(s, w_ref[...].astype(BF16)) + b_ref[...]


def _ada(c16, w_ada, b_ada):
    n = w_ada.shape[1]
    bn = 512
    return pl.pallas_call(
        _ada_kernel,
        grid=(n // bn,),
        in_specs=[_const_spec((16, D_MODEL)),
                  pl.BlockSpec((D_MODEL, bn), lambda j: (0, j)),
                  pl.BlockSpec((1, bn), lambda j: (0, j))],
        out_specs=pl.BlockSpec((16, bn), lambda j: (0, j)),
        out_shape=jax.ShapeDtypeStruct((16, n), F32),
        compiler_params=_params(1),
        name="ada",
    )(c16, w_ada, b_ada.reshape(1, n))


def _inproj_kernel(x_ref, sh_ref, sc_ref, g1_ref, tab_ref, w_ref, qg_ref, wuq_ref, kvg_ref, wk_ref, wvt_ref,
                   wd_ref, bd_ref, *out_refs, with_q):
    if with_q:
        q_ref, k_ref, v_ref, gq_ref, gk_ref, gv_ref, g_ref = out_refs
    else:
        k_ref, v_ref, gk_ref, gv_ref, g_ref = out_refs
    x = x_ref[0]
    h = (_rms(x, g1_ref[...]) * (1.0 + sc_ref[0]) + sh_ref[0]).astype(BF16)

    if with_q:
        zqm = _dot(h, w_ref[:, A_Q[0]:A_MISC[1]])
        misc = zqm[:, A_MISC[0]:]
        qn = _rms(zqm[:, :A_Q[1]], qg_ref[...]).astype(BF16)
        qf = _dot(qn, wuq_ref[...])
        tq = tab_ref[:, 0:HEAD_PAD]
        for hh in range(MLA_HEADS):
            sl = slice(hh * HEAD_PAD, (hh + 1) * HEAD_PAD)
            q_ref[0, :, sl] = (qf[:, sl] * tq).astype(BF16)
        gq_ref[0] = (_dot(h, w_ref[:, A_GQ[0]:A_GQ[1]]) * GLA_QSCALE).astype(BF16)

    zkv = _dot(h, w_ref[:, A_KV[0]:A_KV[1]])
    kvn = _rms(zkv, kvg_ref[...]).astype(BF16)
    if not with_q:
        misc = _dot(h, w_ref[:, A_MISC[0]:A_MISC[1]])
    kr = (misc * tab_ref[:, 128:256]
          + pltpu.roll(misc, HEAD_PAD - 8, 1) * tab_ref[:, 256:384]
          + pltpu.roll(misc, 8, 1) * tab_ref[:, 384:512])
    k_ref[0] = _dot(jnp.concatenate([kvn, kr.astype(BF16)], axis=1), wk_ref[...]).astype(BF16)
    vt = _dot_nt(wvt_ref[...], kvn).astype(BF16)
    kt = v_ref.shape[3]
    for sub in range(v_ref.shape[1]):
        v_ref[0, sub] = vt[:, sub * kt:(sub + 1) * kt]

    gk_ref[0] = _dot(h, w_ref[:, A_GK[0]:A_GK[1]]).astype(BF16)
    gv_ref[0] = _dot(h, w_ref[:, A_GV[0]:A_GV[1]]).astype(BF16)

    gp = _dot(misc.astype(BF16), wd_ref[...]) + bd_ref[...]
    g_ref[0] = (jnp.minimum(gp, 0.0) - jnp.log1p(jnp.exp(-jnp.abs(gp)))) * (1.0 / GATE_NORM)


def _inproj(x, mod3, mod_row, g1, tab, w_a, qg, wuq, kvg, wk, wvt, wd, bd, *, tm, with_q):
    b, l, _ = x.shape
    tm = min(tm, l)
    nt = l // tm
    row = (lambda bi: bi) if mod_row is None else (lambda bi: mod_row)
    tok = lambda w: pl.BlockSpec((1, tm, w), lambda bi, i: (bi, i, 0))
    in_specs = [
        tok(D_MODEL),
        pl.BlockSpec((1, 1, D_MODEL), lambda bi, i: (row(bi), 0, 0)),
        pl.BlockSpec((1, 1, D_MODEL), lambda bi, i: (row(bi), 0, 1)),
        _const_spec((1, D_MODEL)),
        pl.BlockSpec((tm, 512), lambda bi, i: (i, 0)),
        _const_spec(w_a.shape), _const_spec(qg.shape), _const_spec(wuq.shape), _const_spec(kvg.shape),
        _const_spec(wk.shape), _const_spec(wvt.shape), _const_spec(wd.shape), _const_spec(bd.shape),
    ]
    sds = lambda w, dt: jax.ShapeDtypeStruct((b, l, w), dt)
    kw = MLA_HEADS * HEAD_PAD
    outs = [(kw, BF16), None, (GLA_KEY, BF16), (GLA_VALUE, BF16), (2 * GLA_KEY, F32)]
    if with_q:
        outs = [(kw, BF16)] + outs[:2] + [(GLA_KEY, BF16)] + outs[2:]
    kt = min(ATT_KEY_TILE, tm)
    vt_spec = pl.BlockSpec((1, tm // kt, MLA_WIDTH, kt), lambda bi, i: (bi, i, 0, 0))
    vt_shape = jax.ShapeDtypeStruct((b, l // kt, MLA_WIDTH, kt), BF16)
    return pl.pallas_call(
        functools.partial(_inproj_kernel, with_q=with_q),
        grid=(b, nt),
        in_specs=in_specs,
        out_specs=[vt_spec if o is None else tok(o[0]) for o in outs],
        out_shape=[vt_shape if o is None else sds(*o) for o in outs],
        compiler_params=_params(2),
        name="inproj_lat" if with_q else "inproj_ctx",
    )(x, mod3, mod3, g1, tab, w_a, qg, wuq, kvg, wk, wvt, wd, bd)


def _attn_kernel(q_ref, kl_ref, kc_ref, vl_ref, vc_ref, o_ref, sa_ref, sb_ref, sc_ref, ma_ref, mb_ref, mc_ref):
    tq = q_ref.shape[1]
    n_lat, _, tk = vl_ref.shape[1:]
    assert n_lat >= 2 and n_lat % 2 == 0

    def scores(buf, k_t):
        dst, mdst = buf
        for hh in range(2):
            cols = slice(hh * HEAD_PAD, (hh + 1) * HEAD_PAD)
            s = _dot_nt(k_t[:, cols], q_ref[0, :, cols])
            dst[hh] = s
            mdst[hh] = jnp.max(s, axis=0, keepdims=True)

    def consume(buf, vt, carry):
        src, msrc = buf
        ones = jnp.ones((ONES_ROWS, vt.shape[1]), BF16)
        out = []
        for hh in range(2):
            m, acc = carry[hh]
            m_new = jnp.maximum(m, msrc[hh])
            p = jnp.exp2(src[hh] - m_new).astype(BF16)
            v_aug = jnp.concatenate([vt[hh * MLA_V_HEAD:(hh + 1) * MLA_V_HEAD], ones], axis=0)
            out.append((m_new, acc * jnp.exp2(m - m_new) + _dot(v_aug, p)))
        return tuple(out)

    buf_a, buf_b, buf_c = (sa_ref, ma_ref), (sb_ref, mb_ref), (sc_ref, mc_ref)

    def k_tile(i):
        return kl_ref[0, pl.ds(pl.multiple_of(i * tk, tk), tk), :]

    def pair(it, carry):
        j = 2 * it
        scores(buf_b, k_tile(j + 1))
        carry = consume(buf_a, vl_ref[0, j], carry)
        scores(buf_a, k_tile(j + 2))
        return consume(buf_b, vl_ref[0, j + 1], carry)

    head0 = (jnp.full((1, tq), -jnp.inf, F32), jnp.zeros((MLA_V_HEAD + ONES_ROWS, tq), F32))
    scores(buf_a, k_tile(0))
    carry = lax.fori_loop(0, (n_lat - 2) // 2, pair, (head0, head0))
    scores(buf_b, k_tile(n_lat - 1))
    carry = consume(buf_a, vl_ref[0, n_lat - 2], carry)
    scores(buf_c, kc_ref[0])
    carry = consume(buf_b, vl_ref[0, n_lat - 1], carry)
    (_, acc0), (_, acc1) = consume(buf_c, vc_ref[0, 0], carry)
    ot = jnp.concatenate([acc[:MLA_V_HEAD] / acc[MLA_V_HEAD:MLA_V_HEAD + 1] for acc in (acc0, acc1)], axis=0)
    o_ref[0] = ot.T.astype(o_ref.dtype)


def _attention(q, k_lat, k_ctx, vt_lat, vt_ctx, *, tq):
    b, l, _ = q.shape
    c = k_ctx.shape[1]
    n_lat, _, tk = vt_lat.shape[1:]
    pairs = MLA_HEADS // 2
    return pl.pallas_call(
        _attn_kernel,
        grid=(b, pairs, l // tq),
        in_specs=[
            pl.BlockSpec((1, tq, 2 * HEAD_PAD), lambda bi, p, i: (bi, i, p)),
            pl.BlockSpec((1, l, 2 * HEAD_PAD), lambda bi, p, i: (bi, 0, p)),
            pl.BlockSpec((1, c, 2 * HEAD_PAD), lambda bi, p, i: (bi, 0, p)),
            pl.BlockSpec((1, n_lat, 2 * MLA_V_HEAD, tk), lambda bi, p, i: (bi, 0, p, 0)),
            pl.BlockSpec((1, 1, 2 * MLA_V_HEAD, c), lambda bi, p, i: (bi, 0, p, 0)),
        ],
        out_specs=pl.BlockSpec((1, tq, 2 * MLA_V_HEAD), lambda bi, p, i: (bi, i, p)),
        out_shape=jax.ShapeDtypeStruct((b, l, MLA_WIDTH), BF16),
        scratch_shapes=[pltpu.VMEM((2, tk, tq), F32), pltpu.VMEM((2, tk, tq), F32), pltpu.VMEM((2, c, tq), F32)]
        + [pltpu.VMEM((2, 1, tq), F32)] * 3,
        compiler_params=_params(3),
        name="attn",
    )(q, k_lat, k_ctx, vt_lat, vt_ctx)


def _gla_kernel(*refs, with_out):
    if with_out:
        (qf_ref, kf_ref, vf_ref, gf_ref, qb_ref, kb_ref, vb_ref, gb_ref, s0_ref,
         of_ref, ob_ref, sout_ref, st_scr, b_scr) = refs
    else:
        kf_ref, vf_ref, gf_ref, kb_ref, vb_ref, gb_ref, s0_ref, sout_ref, st_scr, b_scr = refs
        qf_ref = qb_ref = of_ref = ob_ref = None
    i = pl.program_id(1)
    t = kf_ref.shape[1]
    n_chunks = t // GLA_CHUNK
    grp = min(t, 256)
    dirs = ((qf_ref, kf_ref, vf_ref, gf_ref, of_ref), (qb_ref, kb_ref, vb_ref, gb_ref, ob_ref))

    @pl.when(i == 0)
    def _():
        st_scr[...] = s0_ref[:, 0]

    r = lax.broadcasted_iota(jnp.int32, (grp, grp), 0)
    c = lax.broadcasted_iota(jnp.int32, (grp, grp), 1)
    same = (r // GLA_CHUNK) == (c // GLA_CHUNK)
    for d, refs_d in enumerate(dirs):
        tri = jnp.where(same & ((c >= r) if d else (c <= r)), 1.0, 0.0).astype(BF16)
        for gi in range(t // grp):
            g = refs_d[3][0, gi * grp:(gi + 1) * grp, :]
            hi = g.astype(BF16)
            r1 = g - hi.astype(F32)
            mid = r1.astype(BF16)
            lo = (r1 - mid.astype(F32)).astype(BF16)
            b_scr[d, gi * grp:(gi + 1) * grp, :] = _dot(tri, hi) + _dot(tri, mid) + _dot(tri, lo)

    rr = lax.broadcasted_iota(jnp.int32, (GLA_CHUNK, GLA_CHUNK), 0)
    cc = lax.broadcasted_iota(jnp.int32, (GLA_CHUNK, GLA_CHUNK), 1)

    def chunk(j, _):
        for d, (q_ref, k_ref, v_ref, _, o_ref) in enumerate(dirs):
            n = (n_chunks - 1 - j) if d else j
            keep = (cc >= rr) if d else (cc <= rr)
            edge = 0 if d else GLA_CHUNK - 1
            rows = pl.ds(pl.multiple_of(n * GLA_CHUNK, GLA_CHUNK), GLA_CHUNK)
            for hh in range(GLA_HEADS):
                cols = slice(hh * GLA_HEAD_K, (hh + 1) * GLA_HEAD_K)
                bc = b_scr[d, rows, cols]
                b_all = bc[edge:edge + 1, :]
                kc = k_ref[0, rows, cols].astype(F32)
                vc = v_ref[0, rows, cols]
                kd = (kc * jnp.exp(b_all - bc)).astype(BF16)
                st = st_scr[d, hh]
                if with_out:
                    qe = (q_ref[0, rows, cols].astype(F32) * jnp.exp(bc)).astype(BF16)
                    ke = (kc * jnp.exp(-bc)).astype(BF16)
                    a = jnp.where(keep, _dot_nt(qe, ke), 0.0).astype(BF16)
                    o = _dot(a, vc) + _dot_nt(qe, st.astype(BF16))
                    o_ref[0, rows, cols] = o.astype(o_ref.dtype)
                st_scr[d, hh] = st * jnp.exp(b_all) + _dot_tn(vc, kd)
        return 0

    lax.fori_loop(0, n_chunks, chunk, 0, unroll=4)

    @pl.when(i == pl.num_programs(1) - 1)
    def _():
        sout_ref[:, 0] = st_scr[...]


def _gla(q, k, v, g, s0, *, tile):
    b, l, _ = k.shape
    tile = min(tile, l)
    nt = l // tile
    with_out = q is not None
    fwd = pl.BlockSpec((1, tile, GLA_KEY), lambda bi, i: (bi, i, 0))
    bwd = pl.BlockSpec((1, tile, GLA_KEY), lambda bi, i: (bi, nt - 1 - i, 0))
    g_bwd = pl.BlockSpec((1, tile, GLA_KEY), lambda bi, i: (bi, nt - 1 - i, 1))
    st_spec = pl.BlockSpec((2, 1, GLA_HEADS, GLA_HEAD_V, GLA_HEAD_K), lambda bi, i: (0, bi, 0, 0, 0))
    st_shape = jax.ShapeDtypeStruct((2, b, GLA_HEADS, GLA_HEAD_V, GLA_HEAD_K), F32)
    o_shape = jax.ShapeDtypeStruct((b, l, GLA_VALUE), BF16)
    q_in = (q,) if with_out else ()
    return pl.pallas_call(
        functools.partial(_gla_kernel, with_out=with_out),
        grid=(b, nt),
        in_specs=[fwd] * len(q_in) + [fwd, fwd, fwd] + [bwd] * len(q_in) + [bwd, bwd, g_bwd, st_spec],
        out_specs=([fwd, bwd] if with_out else []) + [st_spec],
        out_shape=([o_shape, o_shape] if with_out else []) + [st_shape],
        scratch_shapes=[pltpu.VMEM((2, GLA_HEADS, GLA_HEAD_V, GLA_HEAD_K), F32),
                        pltpu.VMEM((2, tile, GLA_KEY), F32)],
        compiler_params=_params(2),
        name="gla" if with_out else "gla_states",
    )(*q_in, k, v, g, *q_in, k, v, g, s0)


def _mix_kernel(x_ref, sh_ref, sc_ref, gt_ref, g1_ref, om_ref, of_ref, ob_ref, wb_ref, gng_ref,
                wbm_ref, wbg_ref, wo_ref, o_ref):
    x = x_ref[0]
    h = (_rms(x, g1_ref[...]) * (1.0 + sc_ref[0]) + sh_ref[0]).astype(BF16)
    zr = _dot(h, wb_ref[:, 0:GLA_VALUE])
    og = of_ref[0].astype(F32) + ob_ref[0].astype(F32)
    ys = []
    for hh in range(GLA_HEADS):
        cols = slice(hh * GLA_HEAD_V, (hh + 1) * GLA_HEAD_V)
        r = zr[:, cols]
        ys.append((_rms(og[:, cols], gng_ref[...]) * (r * jax.nn.sigmoid(r))).astype(BF16))
    br_gla = _dot(jnp.concatenate(ys, axis=1), wbg_ref[...])
    br_mla = _dot(om_ref[0], wbm_ref[...])
    g_mla = jax.nn.sigmoid(_dot(h, wb_ref[:, GLA_VALUE:GLA_VALUE + D_MODEL]))
    g_gla = jax.nn.sigmoid(_dot(h, wb_ref[:, GLA_VALUE + D_MODEL:]))
    merged = (g_mla * br_mla + g_gla * br_gla).astype(BF16)
    o_ref[0] = x + gt_ref[0] * _dot(merged, wo_ref[...])


def _mix(x, mod3, g1, o_mla, o_f, o_b, w_b, gng, wbm, wbg, wo, *, tm):
    b, l, _ = x.shape
    tok = lambda w: pl.BlockSpec((1, tm, w), lambda bi, i: (bi, i, 0))
    modc = lambda j: pl.BlockSpec((1, 1, D_MODEL), lambda bi, i: (bi, 0, j))
    return pl.pallas_call(
        _mix_kernel,
        grid=(b, l // tm),
        in_specs=[tok(D_MODEL), modc(0), modc(1), modc(2), _const_spec((1, D_MODEL)),
                  tok(MLA_WIDTH), tok(GLA_VALUE), tok(GLA_VALUE),
                  _const_spec(w_b.shape), _const_spec(gng.shape), _const_spec(wbm.shape),
                  _const_spec(wbg.shape), _const_spec(wo.shape)],
        out_specs=tok(D_MODEL),
        out_shape=jax.ShapeDtypeStruct((b, l, D_MODEL), F32),
        compiler_params=_params(2),
        name="mix",
    )(x, mod3, mod3, mod3, g1, o_mla, o_f, o_b, w_b, gng, wbm, wbg, wo)


FF_CHUNK = 256
FF_DOWN_GROUP = 6


def _ffn_kernel(xm_ref, xp_ref, xn_ref, sh_ref, sc_ref, gt_ref, g2_ref, wv_ref, wg_ref, cw_ref, cb_ref,
                wd_ref, fg_ref, o_ref, he_scr, gate_scr, val_scr, act_scr, acc_scr):
    i = pl.program_id(1)
    tm = xm_ref.shape[1]
    te = tm + 2 * GRID_W
    n_chunks = D_FF // FF_CHUNK

    def normed(x, keep=None):
        h = _rms(x, g2_ref[...]) * (1.0 + sc_ref[0]) + sh_ref[0]
        return (h if keep is None else h * keep).astype(BF16)

    keep_prev = jnp.where(i > 0, 1.0, 0.0).astype(F32)
    keep_next = jnp.where(i < pl.num_programs(1) - 1, 1.0, 0.0).astype(F32)
    he_scr[0:GRID_W] = normed(xp_ref[0], keep_prev)
    he_scr[GRID_W:GRID_W + tm] = normed(xm_ref[0])
    he_scr[GRID_W + tm:te] = normed(xn_ref[0], keep_next)

    col = lax.broadcasted_iota(jnp.int32, (GRID_W, 1), 0)
    has_left = col > 0
    has_right = col < GRID_W - 1

    def project(c):
        cs = slice(c * FF_CHUNK, (c + 1) * FF_CHUNK)
        gate_scr[c % 2] = _dot(he_scr[...], wg_ref[:, cs])
        val_scr[c % 2] = _dot(he_scr[GRID_W:GRID_W + tm], wv_ref[:, cs])

    def activate(c):
        slot = c % 2
        for j in range(tm // GRID_W):
            for lh in range(FF_CHUNK // 128):
                lanes = slice(lh * 128, (lh + 1) * 128)
                wl = slice(c * FF_CHUNK + lh * 128, c * FF_CHUNK + (lh + 1) * 128)
                sums = [None, None, None]
                for dr in range(3):
                    blk = gate_scr[slot, (j + dr) * GRID_W:(j + dr + 1) * GRID_W, lanes]
                    for dc in range(3):
                        term = blk * cw_ref[3 * dr + dc:3 * dr + dc + 1, wl]
                        sums[dc] = term if sums[dc] is None else sums[dc] + term
                conv = (cb_ref[:, wl] + sums[1]
                        + jnp.where(has_left, pltpu.roll(sums[0], 1, 0), 0.0)
                        + jnp.where(has_right, pltpu.roll(sums[2], GRID_W - 1, 0), 0.0))
                gelu = 0.5 * conv * (1.0 + lax.erf(conv * np.float32(2.0 ** -0.5)))
                rows = slice(j * GRID_W, (j + 1) * GRID_W)
                grp, pos = divmod(c, FF_DOWN_GROUP)
                cols = slice(pos * FF_CHUNK + lh * 128, pos * FF_CHUNK + (lh + 1) * 128)
                act_scr[grp % 2, rows, cols] = (gelu * val_scr[slot, rows, lanes]).astype(BF16)

    def down(grp, n):
        r0 = grp * FF_DOWN_GROUP * FF_CHUNK
        part = _dot(act_scr[grp % 2, :, 0:n * FF_CHUNK], wd_ref[r0:r0 + n * FF_CHUNK, :])
        if grp == 0:
            acc_scr[...] = part
        else:
            acc_scr[...] += part

    project(0)
    for s in range(n_chunks):
        if s + 1 < n_chunks:
            project(s + 1)
        activate(s)
        if (s + 1) % FF_DOWN_GROUP == 0 or s == n_chunks - 1:
            down(s // FF_DOWN_GROUP, s % FF_DOWN_GROUP + 1)
    x2 = xm_ref[0] + gt_ref[0] * acc_scr[...]
    o_ref[0] = _rms(x2, fg_ref[...])


def _ffn(x1, mod3, g2, w_up, cw, cb, wd, fg, *, tm):
    b, l, _ = x1.shape
    half = lambda j: pl.BlockSpec((D_MODEL, D_FF), lambda bi, i: (0, j), pipeline_mode=pl.Buffered(1))
    rpt = tm // GRID_W
    n_rows = l // GRID_W
    tok = lambda w: pl.BlockSpec((1, tm, w), lambda bi, i: (bi, i, 0))
    modc = lambda j: pl.BlockSpec((1, 1, D_MODEL), lambda bi, i: (bi, 0, j))
    return pl.pallas_call(
        _ffn_kernel,
        grid=(b, l // tm),
        in_specs=[tok(D_MODEL),
                  pl.BlockSpec((1, GRID_W, D_MODEL), lambda bi, i: (bi, jnp.maximum(i * rpt - 1, 0), 0)),
                  pl.BlockSpec((1, GRID_W, D_MODEL),
                               lambda bi, i: (bi, jnp.minimum((i + 1) * rpt, n_rows - 1), 0)),
                  modc(3), modc(4), modc(5), _const_spec((1, D_MODEL)),
                  half(0), half(1), _const_spec(cw.shape), _const_spec(cb.shape),
                  _const_spec(wd.shape), _const_spec((1, D_MODEL))],
        out_specs=tok(D_MODEL),
        out_shape=jax.ShapeDtypeStruct((b, l, D_MODEL), F32),
        scratch_shapes=[pltpu.VMEM((tm + 2 * GRID_W, D_MODEL), BF16),
                        pltpu.VMEM((2, tm + 2 * GRID_W, FF_CHUNK), F32),
                        pltpu.VMEM((2, tm, FF_CHUNK), F32),
                        pltpu.VMEM((2, tm, FF_DOWN_GROUP * FF_CHUNK), BF16),
                        pltpu.VMEM((tm, D_MODEL), F32)],
        compiler_params=_params(2),
        name="ffn",
    )(x1, x1, x1, mod3, mod3, mod3, g2, w_up, w_up, cw, cb, wd, fg)


def _rope_tables(length):
    f32 = np.float32
    t = np.arange(length)
    inv_freq = (f32(ROPE_THETA) ** (-np.arange(0, ROPE_AXIS_DIM, 2, dtype=f32) / f32(ROPE_AXIS_DIM))).astype(f32)
    row = (t // GRID_W).astype(f32)[:, None] * inv_freq
    col = (t % GRID_W).astype(f32)[:, None] * inv_freq
    ang = np.concatenate([row, row, col, col], axis=1)
    cos, sin = np.cos(ang).astype(f32), np.sin(ang).astype(f32)
    z8 = np.zeros((length, ROPE_AXIS_DIM // 2), f32)
    pad = np.zeros((length, HEAD_PAD - MLA_ROPE), f32)
    tq = f32(MLA_SCALE * LOG2E) * np.concatenate([np.ones((length, MLA_NOPE), f32), cos, sin], axis=1)
    s_row, s_col = np.sin(row).astype(f32), np.sin(col).astype(f32)
    s_next = np.concatenate([-s_row, z8, -s_col, z8, pad], axis=1)
    s_prev = np.concatenate([z8, s_row, z8, s_col, pad], axis=1)
    return jnp.asarray(np.concatenate([tq, np.concatenate([cos, pad], axis=1), s_next, s_prev], axis=1))


def _ctx_tables(length):
    one = np.concatenate([np.ones((length, MLA_ROPE), np.float32),
                          np.zeros((length, HEAD_PAD - MLA_ROPE), np.float32)], 1)
    z = np.zeros((length, HEAD_PAD), np.float32)
    return jnp.asarray(np.concatenate([z, one, z, z], axis=1))


def _rot_partner(w):
    h = ROPE_AXIS_DIM // 2
    r1, r2, c1, c2 = w[..., 0:h], w[..., h:2 * h], w[..., 2 * h:3 * h], w[..., 3 * h:4 * h]
    return jnp.concatenate([-r2, r1, -c2, c1], axis=-1)


def _layout_weights(w_in, w_uq, w_ukv, w_decay, b_decay):
    offs = np.cumsum((0, MLA_Q_LORA, MLA_KV_LORA, MLA_ROPE, GLA_KEY, GLA_KEY, GLA_VALUE, GLA_VALUE,
                      2 * GATE_RANK, 2 * D_MODEL))
    part = lambda j: w_in[:, offs[j]:offs[j + 1]]
    zc = lambda n: jnp.zeros((D_MODEL, n), F32)
    w_a = jnp.concatenate([part(0), part(2), part(7), zc(HEAD_PAD - MLA_ROPE - 2 * GATE_RANK), part(1),
                           part(3), part(4), part(5)], axis=1).astype(BF16)
    w_b = jnp.concatenate([part(6), part(8)], axis=1).astype(BF16)

    uq = w_uq.reshape(MLA_Q_LORA, MLA_HEADS, MLA_NOPE + MLA_ROPE)
    rope = uq[..., MLA_NOPE:]
    wuq = jnp.concatenate([uq, _rot_partner(rope)], axis=-1).reshape(MLA_Q_LORA, MLA_HEADS * HEAD_PAD)

    ukv = w_ukv.reshape(MLA_KV_LORA, MLA_HEADS, MLA_NOPE + MLA_V_HEAD)
    k_cols = jnp.concatenate([ukv[..., :MLA_NOPE], jnp.zeros((MLA_KV_LORA, MLA_HEADS, HEAD_PAD - MLA_NOPE), F32)],
                             axis=-1).reshape(MLA_KV_LORA, MLA_HEADS * HEAD_PAD)
    v_cols = ukv[..., MLA_NOPE:].reshape(MLA_KV_LORA, MLA_WIDTH)
    eye = jnp.eye(HEAD_PAD, MLA_ROPE, dtype=F32)
    route = jnp.concatenate([jnp.zeros((HEAD_PAD, MLA_NOPE), F32), eye, eye], axis=1)
    route = jnp.tile(route, (1, MLA_HEADS))
    wk = jnp.concatenate([k_cols, route], axis=0)

    wd = jnp.zeros((HEAD_PAD, 2 * GLA_KEY), F32)
    wd = wd.at[MLA_ROPE:MLA_ROPE + GATE_RANK, :GLA_KEY].set(w_decay[0])
    wd = wd.at[MLA_ROPE + GATE_RANK:MLA_ROPE + 2 * GATE_RANK, GLA_KEY:].set(w_decay[1])
    bd = b_decay.reshape(1, 2 * GLA_KEY)
    return w_a, w_b, wuq.astype(BF16), wk.astype(BF16), v_cols.T.astype(BF16), wd.astype(BF16), bd


def kernel(x, c, ctx, c_ctx, w_ada, b_ada, norm1_g, w_in, q_norm_g, w_uq, kv_norm_g, w_ukv, gla_w_decay,
           gla_b_decay, gla_norm_g, w_br_mla, w_br_gla, w_out, norm2_g, w_up, conv_w, conv_b, w_down, final_g):
    b, l, _ = x.shape
    n_ctx = ctx.shape[1]
    assert w_ada.shape[0] == 1, "single layer: context tokens are never updated"
    row2 = lambda a: a.reshape(1, -1)

    c16 = jnp.concatenate([c, c_ctx[None], jnp.zeros((16 - b - 1, D_MODEL), F32)], axis=0)
    mod3 = _ada(c16, w_ada[0], b_ada[0]).reshape(16, 1, 6 * D_MODEL)

    w_a, w_b, wuq, wk, wvt, wd, bd = _layout_weights(w_in[0], w_uq[0], w_ukv[0], gla_w_decay[0], gla_b_decay[0])
    g1 = row2(norm1_g[0])
    proj = functools.partial(_inproj, g1=g1, w_a=w_a, qg=row2(q_norm_g[0]), wuq=wuq, kvg=row2(kv_norm_g[0]),
                             wk=wk, wvt=wvt, wd=wd, bd=bd, tm=TOKEN_TILE)
    q, k, v, gq, gk, gv, g = proj(x, mod3, None, tab=_rope_tables(l), with_q=True)
    k_c, v_c, gk_c, gv_c, g_c = proj(ctx, mod3, b, tab=_ctx_tables(n_ctx), with_q=False)

    o_mla = _attention(q, k, k_c, v, v_c, tq=1024)

    s0 = jnp.zeros((2, b, GLA_HEADS, GLA_HEAD_V, GLA_HEAD_K), F32)
    (s_ctx,) = _gla(None, gk_c, gv_c, g_c, s0, tile=256)
    o_f, o_b, _ = _gla(gq, gk, gv, g, s_ctx, tile=512)

    x1 = _mix(x, mod3, g1, o_mla, o_f, o_b, w_b, row2(gla_norm_g[0]), w_br_mla[0].astype(BF16),
              w_br_gla[0].astype(BF16), w_out[0].astype(BF16), tm=TOKEN_TILE)

    return _ffn(x1, mod3, row2(norm2_g[0]), w_up[0].astype(BF16), conv_w[0].reshape(9, D_FF),
                row2(conv_b[0]), w_down[0].astype(BF16), row2(final_g), tm=FFN_TILE)
```
